```python
import math
import jax, jax.numpy as jnp
from jax import lax
import numpy as np

D_MODEL = 1024
BATCH = 8
SEQ = 4096
DEPTH = 2

HEAD_DIM = 64
N_HEADS_A = 8
N_HEADS_B = 8
N_HEADS = N_HEADS_A + N_HEADS_B
D_MIX = N_HEADS * HEAD_DIM
D_A = N_HEADS_A * HEAD_DIM
D_B = N_HEADS_B * HEAD_DIM
KV_RANK_A = 128
N_KV_A = 2
R_A = N_HEADS_A // N_KV_A
N_IDX_HEADS = 8
IDX_DIM = 64
TOPK_A = 256
N_KV_B = 2
R_B = N_HEADS_B // N_KV_B
D_KV_B = N_KV_B * HEAD_DIM
CMP_BLOCK = 32
CMP_STRIDE = 16
SLC_BLOCK = 64
N_SLC = 16
WINDOW = 512
N_BUCKETS = 32
MAX_DISTANCE = 1024
Q_BLOCK_A = 128
Q_BLOCK_B = 64
EPS = 1e-6
NEG_INF = -1e30
FORCE_BONUS = 1e4
ATTN_SCALE = HEAD_DIM ** -0.5
IDX_SCALE = IDX_DIM ** -0.5
IDX_W_SCALE = N_IDX_HEADS ** -0.5

PROJ_SIZES = (
    D_A, KV_RANK_A, N_IDX_HEADS * IDX_DIM, IDX_DIM, N_IDX_HEADS, D_A,
    D_B, D_KV_B, D_KV_B, D_KV_B, D_KV_B, D_KV_B, D_KV_B, N_HEADS_B * 3, D_B,
)
D_IN = sum(PROJ_SIZES)
PROJ_SPLITS = tuple(int(v) for v in np.cumsum(PROJ_SIZES)[:-1])

kernel_name = 'hybrid_dsa_nsa_parallel_heads'


def rms_norm(x, g):
    xf = x.astype(jnp.float32)
    y = xf * lax.rsqrt(jnp.mean(xf * xf, axis=-1, keepdims=True) + EPS)
    return (y * g.astype(jnp.float32)).astype(x.dtype)


def masked_softmax(logits, mask):
    p = jax.nn.softmax(jnp.where(mask, logits, NEG_INF), axis=-1)
    return jnp.where(mask, p, 0.0)


def t5_bucket(dist):
    n = jnp.maximum(dist, 0)
    max_exact = N_BUCKETS // 2
    nf = jnp.maximum(n, 1).astype(jnp.float32)
    log_b = max_exact + (jnp.log(nf / max_exact) / math.log(MAX_DISTANCE / max_exact)
                         * (N_BUCKETS - max_exact)).astype(jnp.int32)
    return jnp.where(n < max_exact, n, jnp.minimum(log_b, N_BUCKETS - 1))


def band_bias(bias_b, dist):
    return bias_b[t5_bucket(dist)].transpose(0, 2, 3, 1)[None].astype(jnp.float32)


def overlap_map(n_c, n_sb):
    c_start = np.arange(n_c)[:, None] * CMP_STRIDE
    b_start = np.arange(n_sb)[None, :] * SLC_BLOCK
    ov = np.minimum(c_start + CMP_BLOCK, b_start + SLC_BLOCK) - np.maximum(c_start, b_start)
    return (np.maximum(ov, 0) / CMP_BLOCK).astype(np.float32)


def compress(k, pos_emb, w1, w2):
    B, L = k.shape[:2]
    n_c = (L - CMP_BLOCK) // CMP_STRIDE + 1
    idx = np.arange(n_c)[:, None] * CMP_STRIDE + np.arange(CMP_BLOCK)[None, :]
    blocks = k[:, idx] + pos_emb[None, None, :, None, :]
    blocks = blocks.transpose(0, 1, 3, 2, 4).reshape(B, n_c, N_KV_B, CMP_BLOCK * HEAD_DIM)
    return jax.nn.gelu(blocks @ w1) @ w2


def dsa_attention(q, k, v, q_idx, k_idx, w_idx, bias_a):
    B, L = q.shape[:2]
    k_top = min(TOPK_A, L // 4)
    n_qb = L // Q_BLOCK_A
    s_pos = jnp.arange(L)
    b_idx = jnp.arange(B)[:, None, None]

    def block(i):
        q0 = i * Q_BLOCK_A
        t = q0 + jnp.arange(Q_BLOCK_A)
        qa = lax.dynamic_slice_in_dim(q, q0, Q_BLOCK_A, axis=1).reshape(B, Q_BLOCK_A, N_KV_A, R_A, HEAD_DIM)
        qi = lax.dynamic_slice_in_dim(q_idx, q0, Q_BLOCK_A, axis=1)
        wi = lax.dynamic_slice_in_dim(w_idx, q0, Q_BLOCK_A, axis=1).astype(jnp.float32) * IDX_W_SCALE
        dots = jnp.einsum('bqhd,bsd->bqhs', qi, k_idx).astype(jnp.float32) * IDX_SCALE
        score = jnp.einsum('bqh,bqhs->bqs', wi, jax.nn.relu(dots))
        score = jnp.where((s_pos[None, :] <= t[:, None])[None], score, NEG_INF)
        _, sel = lax.top_k(score, k_top)
        k_sel = k[b_idx, sel]
        v_sel = v[b_idx, sel]
        dist = t[None, :, None] - sel
        logits = jnp.einsum('bqgrd,bqkgd->bqgrk', qa, k_sel).astype(jnp.float32) * ATTN_SCALE
        bias = bias_a[t5_bucket(dist)].reshape(B, Q_BLOCK_A, k_top, N_KV_A, R_A).transpose(0, 1, 3, 4, 2)
        p = masked_softmax(logits + bias.astype(jnp.float32), (dist >= 0)[:, :, None, None, :])
        o = jnp.einsum('bqgrk,bqkgd->bqgrd', p.astype(v_sel.dtype), v_sel)
        return o.reshape(B, Q_BLOCK_A, D_A)

    out = lax.map(block, jnp.arange(n_qb))
    return out.transpose(1, 0, 2, 3).reshape(B, L, D_A)


def nsa_attention(q, k_c, v_c, k_s, v_s, k_w, v_w, gates, bias_b):
    B, L = q.shape[:2]
    n_c = k_c.shape[1]
    n_sb = L // SLC_BLOCK
    n_sel = min(N_SLC, n_sb)
    n_qb = L // Q_BLOCK_B
    end_c = jnp.asarray(np.arange(n_c) * CMP_STRIDE + CMP_BLOCK - 1)
    slc_map = jnp.asarray(overlap_map(n_c, n_sb))
    blk = jnp.arange(n_sb)
    ks_blk = k_s.reshape(B, n_sb, SLC_BLOCK, N_KV_B, HEAD_DIM).transpose(0, 3, 1, 2, 4)
    vs_blk = v_s.reshape(B, n_sb, SLC_BLOCK, N_KV_B, HEAD_DIM).transpose(0, 3, 1, 2, 4)
    pad = ((0, 0), (WINDOW, 0), (0, 0), (0, 0))
    kw_pad = jnp.pad(k_w, pad)
    vw_pad = jnp.pad(v_w, pad)
    b_i = jnp.arange(B)[:, None, None, None]
    g_i = jnp.arange(N_KV_B)[None, None, :, None]
    in_blk = jnp.arange(SLC_BLOCK)
    n_tok = n_sel * SLC_BLOCK

    def block(i):
        q0 = i * Q_BLOCK_B
        t = q0 + jnp.arange(Q_BLOCK_B)
        qb = lax.dynamic_slice_in_dim(q, q0, Q_BLOCK_B, axis=1).reshape(B, Q_BLOCK_B, N_KV_B, R_B, HEAD_DIM)
        gb = lax.dynamic_slice_in_dim(gates, q0, Q_BLOCK_B, axis=1).reshape(B, Q_BLOCK_B, N_KV_B, R_B, 3)
        dist_c = t[:, None] - end_c[None, :]
        logits_c = jnp.einsum('bqgrd,bcgd->bqgrc', qb, k_c).astype(jnp.float32) * ATTN_SCALE
        p_c = masked_softmax(logits_c + band_bias(bias_b, dist_c), (dist_c >= 0)[None, :, None, None, :])
        o_c = jnp.einsum('bqgrc,bcgd->bqgrd', p_c.astype(v_c.dtype), v_c)
        imp = jnp.einsum('bqgrc,cn->bqgn', p_c, slc_map)
        jt = t // SLC_BLOCK
        forced = (blk[None, :] == 0) | (blk[None, :] == jt[:, None]) | (blk[None, :] == jt[:, None] - 1)
        imp = imp + jnp.where(forced, FORCE_BONUS, 0.0)[None, :, None, :]
        imp = jnp.where((blk[None, :] * SLC_BLOCK <= t[:, None])[None, :, None, :], imp, NEG_INF)
        _, sel = lax.top_k(imp, n_sel)
        k_sel = ks_blk[b_i, g_i, sel].reshape(B, Q_BLOCK_B, N_KV_B, n_tok, HEAD_DIM)
        v_sel = vs_blk[b_i, g_i, sel].reshape(B, Q_BLOCK_B, N_KV_B, n_tok, HEAD_DIM)
        pos = (sel[..., None] * SLC_BLOCK + in_blk).reshape(B, Q_BLOCK_B, N_KV_B, n_tok)
        dist_s = t[None, :, None, None] - pos
        logits_s = jnp.einsum('bqgrd,bqgnd->bqgrn', qb, k_sel).astype(jnp.float32) * ATTN_SCALE
        bias_s = bias_b[t5_bucket(dist_s), g_i].transpose(0, 1, 2, 4, 3)
        p_s = masked_softmax(logits_s + bias_s.astype(jnp.float32), (dist_s >= 0)[:, :, :, None, :])
        o_s = jnp.einsum('bqgrn,bqgnd->bqgrd', p_s.astype(v_sel.dtype), v_sel)
        s = q0 - WINDOW + jnp.arange(WINDOW + Q_BLOCK_B)
        kw = lax.dynamic_slice_in_dim(kw_pad, q0, WINDOW + Q_BLOCK_B, axis=1)
        vw = lax.dynamic_slice_in_dim(vw_pad, q0, WINDOW + Q_BLOCK_B, axis=1)
        dist_w = t[:, None] - s[None, :]
        mask_w = (dist_w >= 0) & (dist_w < WINDOW) & (s[None, :] >= 0)
        logits_w = jnp.einsum('bqgrd,bsgd->bqgrs', qb, kw).astype(jnp.float32) * ATTN_SCALE
        p_w = masked_softmax(logits_w + band_bias(bias_b, dist_w), mask_w[None, :, None, None, :])
        o_w = jnp.einsum('bqgrs,bsgd->bqgrd', p_w.astype(vw.dtype), vw)
        o = gb[..., 0:1] * o_c + gb[..., 1:2] * o_s + gb[..., 2:3] * o_w
        return o.reshape(B, Q_BLOCK_B, D_B)

    out = lax.map(block, jnp.arange(n_qb))
    return out.transpose(1, 0, 2, 3).reshape(B, L, D_B)


def setup_inputs(seed: int = 0) -> dict:
    key = jax.random.key(seed)
    ks = jax.random.split(key, 24)
    f32 = jnp.float32

    def nrm(k, shape, scale):
        return jax.random.normal(k, shape, f32) * scale

    def gain(k, shape):
        return 1.0 + 0.01 * jax.random.normal(k, shape, f32)

    return {
        'x': nrm(ks[0], (BATCH, SEQ, D_MODEL), 1.0),
        'rel_bias': nrm(ks[1], (N_BUCKETS, N_HEADS), 0.5),
        'norm_g': gain(ks[2], (DEPTH, D_MODEL)),
        'w_in': nrm(ks[3], (DEPTH, D_MODEL, D_IN), D_MODEL ** -0.5),
        'b_gate_b': nrm(ks[4], (DEPTH, N_HEADS_B * 3), 0.1),
        'kv_norm_a': gain(ks[5], (DEPTH, KV_RANK_A)),
        'w_kv_up_a': nrm(ks[6], (DEPTH, KV_RANK_A, 2 * N_KV_A * HEAD_DIM), KV_RANK_A ** -0.5),
        'q_norm_a': gain(ks[7], (DEPTH, HEAD_DIM)),
        'k_norm_a': gain(ks[8], (DEPTH, HEAD_DIM)),
        'q_norm_b': gain(ks[9], (DEPTH, HEAD_DIM)),
        'k_norm_cmp': gain(ks[10], (DEPTH, HEAD_DIM)),
        'k_norm_slc': gain(ks[11], (DEPTH, HEAD_DIM)),
        'k_norm_win': gain(ks[12], (DEPTH, HEAD_DIM)),
        'cmp_pos_k': nrm(ks[13], (DEPTH, CMP_BLOCK, HEAD_DIM), 0.2),
        'cmp_pos_v': nrm(ks[14], (DEPTH, CMP_BLOCK, HEAD_DIM), 0.2),
        'w_cmp_k1': nrm(ks[15], (DEPTH, CMP_BLOCK * HEAD_DIM, HEAD_DIM), (CMP_BLOCK * HEAD_DIM) ** -0.5),
        'w_cmp_k2': nrm(ks[16], (DEPTH, HEAD_DIM, HEAD_DIM), HEAD_DIM ** -0.5),
        'w_cmp_v1': nrm(ks[17], (DEPTH, CMP_BLOCK * HEAD_DIM, HEAD_DIM), (CMP_BLOCK * HEAD_DIM) ** -0.5),
        'w_cmp_v2': nrm(ks[18], (DEPTH, HEAD_DIM, HEAD_DIM), HEAD_DIM ** -0.5),
        'w_out': nrm(ks[19], (DEPTH, D_MIX, D_MODEL), D_MIX ** -0.5),
    }


def reference(x, rel_bias, norm_g, w_in, b_gate_b, kv_norm_a, w_kv_up_a, q_norm_a, k_norm_a,
              q_norm_b, k_norm_cmp, k_norm_slc, k_norm_win, cmp_pos_k, cmp_pos_v,
              w_cmp_k1, w_cmp_k2, w_cmp_v1, w_cmp_v2, w_out):
    B, L, _ = x.shape
    bias_a = rel_bias[:, :N_HEADS_A]
    bias_b = rel_bias[:, N_HEADS_A:].reshape(N_BUCKETS, N_KV_B, R_B)
    kv_shape = (B, L, N_KV_B, HEAD_DIM)
    for l in range(DEPTH):
        h = rms_norm(x, norm_g[l])
        proj = h @ w_in[l]
        (q_a, c_kv, q_idx, k_idx, w_idx, z_a,
         q_b, k_cmp, v_cmp, k_slc, v_slc, k_win, v_win, g_b, z_b) = jnp.split(proj, PROJ_SPLITS, axis=-1)
        q_a = rms_norm(q_a.reshape(B, L, N_HEADS_A, HEAD_DIM), q_norm_a[l])
        kv_a = (rms_norm(c_kv, kv_norm_a[l]) @ w_kv_up_a[l]).reshape(B, L, 2 * N_KV_A, HEAD_DIM)
        k_a, v_a = jnp.split(kv_a, 2, axis=2)
        k_a = rms_norm(k_a, k_norm_a[l])
        o_a = dsa_attention(q_a, k_a, v_a, q_idx.reshape(B, L, N_IDX_HEADS, IDX_DIM), k_idx, w_idx, bias_a)
        q_b = rms_norm(q_b.reshape(B, L, N_HEADS_B, HEAD_DIM), q_norm_b[l])
        k_c = rms_norm(compress(k_cmp.reshape(kv_shape), cmp_pos_k[l], w_cmp_k1[l], w_cmp_k2[l]), k_norm_cmp[l])
        v_c = compress(v_cmp.reshape(kv_shape), cmp_pos_v[l], w_cmp_v1[l], w_cmp_v2[l])
        k_s = rms_norm(k_slc.reshape(kv_shape), k_norm_slc[l])
        k_w = rms_norm(k_win.reshape(kv_shape), k_norm_win[l])
        gates = jax.nn.sigmoid((g_b + b_gate_b[l]).astype(jnp.float32)).astype(x.dtype).reshape(B, L, N_HEADS_B, 3)
        o_b = nsa_attention(q_b, k_c, v_c, k_s, v_slc.reshape(kv_shape), k_w, v_win.reshape(kv_shape), gates, bias_b)
        y = jnp.concatenate([o_a * jax.nn.silu(z_a), o_b * jax.nn.silu(z_b)], axis=-1)
        x = x + y @ w_out[l]
    return x
```

```python
import functools
import math

import jax
import jax.numpy as jnp
import numpy as np
from jax import lax
from jax.experimental import pallas as pl
from jax.experimental.pallas import tpu as pltpu

D_MODEL = 1024
HEAD_DIM = 64
N_HEADS_A = 8
N_HEADS_B = 8
D_A = N_HEADS_A * HEAD_DIM
D_B = N_HEADS_B * HEAD_DIM
KV_RANK_A = 128
N_KV_A = 2
N_IDX_HEADS = 8
IDX_DIM = 64
TOPK_A = 256
N_KV_B = 2
D_KV_B = N_KV_B * HEAD_DIM
CMP_BLOCK = 32
CMP_STRIDE = 16
SLC_BLOCK = 64
N_SLC = 16
WINDOW = 512
N_BUCKETS = 32
MAX_DISTANCE = 1024
EPS = 1e-6
NEG_INF = -1e30
FORCE_BONUS = 1e4
ATTN_SCALE = HEAD_DIM ** -0.5
IDX_SCALE = IDX_DIM ** -0.5
IDX_W_SCALE = N_IDX_HEADS ** -0.5

PROJ_SIZES = (
    D_A, KV_RANK_A, N_IDX_HEADS * IDX_DIM, IDX_DIM, N_IDX_HEADS, D_A,
    D_B, D_KV_B, D_KV_B, D_KV_B, D_KV_B, D_KV_B, D_KV_B, N_HEADS_B * 3, D_B,
)
PROJ_NAMES = ("q_a", "c_kv", "q_idx", "k_idx", "w_idx", "z_a",
              "q_b", "k_cmp", "v_cmp", "k_slc", "v_slc", "k_win", "v_win", "g_b", "z_b")
_OFFS = np.concatenate([[0], np.cumsum(PROJ_SIZES)])
PROJ_COLS = {n: (int(_OFFS[i]), int(_OFFS[i + 1])) for i, n in enumerate(PROJ_NAMES)}

TQ = 256
TS = 256
T_PROJ = 512
T_OUT = 512
FAR_TILES = 5
VMEM_LIMIT = 56 * 1024 * 1024
INT_MIN = -(2 ** 31)

F32 = jnp.float32
BF16 = jnp.bfloat16
NT_DIMS = (((1,), (1,)), ((), ()))

ROW_ORDER = ("c_kv", "z_a", "z_b", "k_cmp", "v_cmp", "k_slc", "k_win", "k_idx")
ROW_WIDTH = 1792
COL_ORDER = ("q_a", "q_idx", "q_b", "v_slc", "v_win", "g_b", "w_idx")


def _layout(order):
    offs, o = {}, 0
    for n in order:
        w = PROJ_COLS[n][1] - PROJ_COLS[n][0]
        offs[n] = (o, o + w)
        o += w
    return offs, o


ROW_OFFS, _ROW_USED = _layout(ROW_ORDER)
COL_OFFS, COL_ROWS = _layout(COL_ORDER)


def _dot(a, b):
    return jnp.dot(a, b, preferred_element_type=F32)


def _dot_nt(a, b):
    return lax.dot_general(a, b, NT_DIMS, preferred_element_type=F32)


def _rms_rows(a, gain):
    ms = jnp.mean(a * a, axis=-1, keepdims=True)
    return a * lax.rsqrt(ms + EPS) * gain


def _group_rms_rows(a, gain):
    lane = lax.broadcasted_iota(jnp.int32, a.shape, 1)
    sq = a * a
    lo = lane < HEAD_DIM
    s_lo = jnp.sum(jnp.where(lo, sq, 0.0), axis=-1, keepdims=True)
    s_hi = jnp.sum(jnp.where(lo, 0.0, sq), axis=-1, keepdims=True)
    ms = jnp.where(lo, s_lo, s_hi) * (1.0 / HEAD_DIM)
    return a * lax.rsqrt(ms + EPS) * gain


def _proj_kernel(x_ref, g_ref, wrow_ref, wcol_ref, kvn_ref, wup_ref, wupv_ref, kna_ref, kns_ref, knw_ref,
                 qna_ref, qnb_ref, bg_ref,
                 ka_ref, kidx_ref, ks_ref, kw_ref, kcmp_ref, vcmp_ref, sz_ref,
                 qa_ref, qi_ref, qb_ref, va_ref, vs_ref, vw_ref, gt_ref, wt_ref):
    t = x_ref.shape[0]
    h = _rms_rows(x_ref[...], g_ref[...]).astype(BF16)
    pr = _dot(h, wrow_ref[...])

    def rows(name):
        a, b = ROW_OFFS[name]
        return pr[:, a:b]

    ckvn = _rms_rows(rows("c_kv"), kvn_ref[...]).astype(BF16)
    k_up = _dot(ckvn, wup_ref[...])
    ka_ref[...] = _group_rms_rows(k_up, kna_ref[...]).astype(BF16)
    va_ref[...] = _dot_nt(wupv_ref[...], ckvn).astype(BF16)
    za, zb = ROW_OFFS["z_a"][0], ROW_OFFS["z_b"][1]
    z = pr[:, za:zb]
    sz_ref[...] = z * jax.nn.sigmoid(z)
    kcmp_ref[...] = rows("k_cmp")
    vcmp_ref[...] = rows("v_cmp")
    ks_ref[...] = _group_rms_rows(rows("k_slc"), kns_ref[...]).astype(BF16)
    kw_ref[...] = _group_rms_rows(rows("k_win"), knw_ref[...]).astype(BF16)
    kidx_ref[...] = rows("k_idx").astype(BF16)

    pt = _dot_nt(wcol_ref[...], h)

    def cols(name):
        a, b = COL_OFFS[name]
        return pt[a:b, :]

    def qnorm_padded(q, gain_col):
        q = q.reshape(N_HEADS_A, HEAD_DIM, t)
        ms = jnp.mean(q * q, axis=1, keepdims=True)
        qn = (q * lax.rsqrt(ms + EPS) * gain_col[None] * ATTN_SCALE).astype(BF16)
        zeros = jnp.zeros((HEAD_DIM, t), BF16)
        parts = []
        for hh in range(N_HEADS_A):
            parts += [qn[hh], zeros] if hh < N_HEADS_A // N_KV_A else [zeros, qn[hh]]
        return jnp.concatenate(parts, axis=0)

    qa_ref[...] = qnorm_padded(cols("q_a"), qna_ref[...])
    qb_ref[...] = qnorm_padded(cols("q_b"), qnb_ref[...])
    qi_ref[...] = (cols("q_idx") * IDX_SCALE).astype(BF16)
    vs_ref[...] = cols("v_slc").astype(BF16)
    vw_ref[...] = cols("v_win").astype(BF16)
    gt_ref[...] = jax.nn.sigmoid(cols("g_b") + bg_ref[...])
    wt_ref[...] = cols("w_idx") * IDX_W_SCALE


def _const_spec(shape):
    nd = len(shape)
    return pl.BlockSpec(shape, lambda *_: (0,) * nd)


def _proj_call(x, g, wrow, wcol, kvn, wup, wupv, kna, kns, knw, qna, qnb, bg):
    b, l, _ = x.shape
    t = min(T_PROJ, l)
    grid = (b, l // t)
    row = lambda w: pl.BlockSpec((None, t, w), lambda i, j: (i, j, 0))
    col = lambda r: pl.BlockSpec((None, r, t), lambda i, j: (i, 0, j))
    rs = lambda w, dt: jax.ShapeDtypeStruct((b, l, w), dt)
    cs = lambda r, dt: jax.ShapeDtypeStruct((b, r, l), dt)
    consts = (g, wrow, wcol, kvn, wup, wupv, kna, kns, knw, qna, qnb, bg)
    return pl.pallas_call(
        _proj_kernel,
        grid=grid,
        in_specs=[row(D_MODEL)] + [_const_spec(c.shape) for c in consts],
        out_specs=[row(128), row(IDX_DIM), row(128), row(128), row(128), row(128), row(D_A + D_B),
                   col(2 * D_A), col(N_IDX_HEADS * IDX_DIM), col(2 * D_B), col(128), col(128), col(128),
                   col(N_HEADS_B * 3), col(N_IDX_HEADS)],
        out_shape=[rs(128, BF16), rs(IDX_DIM, BF16), rs(128, BF16), rs(128, BF16), rs(128, F32), rs(128, F32),
                   rs(D_A + D_B, F32),
                   cs(2 * D_A, BF16), cs(N_IDX_HEADS * IDX_DIM, BF16), cs(2 * D_B, BF16), cs(128, BF16),
                   cs(128, BF16), cs(128, BF16), cs(N_HEADS_B * 3, F32), cs(N_IDX_HEADS, F32)],
        compiler_params=pltpu.CompilerParams(dimension_semantics=("arbitrary", "arbitrary"),
                                             vmem_limit_bytes=VMEM_LIMIT),
        name="proj",
    )(x, *consts)


def _compress_kernel(kr_ref, vr_ref, pk_ref, pv_ref, wk1_ref, wv1_ref, wk2_ref, wv2_ref, knc_ref,
                     kc_ref, vct_ref):
    n = kr_ref.shape[0]
    row = lax.broadcasted_iota(jnp.int32, (n, 128), 0)
    valid = row < n - 1

    def branch(r_ref, p_ref, w1_ref, w2_ref):
        r = r_ref[...]
        a = _dot((r + p_ref[0:1, :]).astype(BF16), w1_ref[0])
        b = _dot((r + p_ref[1:2, :]).astype(BF16), w1_ref[1])
        pre = a + pltpu.roll(b, n - 1, axis=0)
        return _dot(jax.nn.gelu(pre).astype(BF16), w2_ref[...])

    kc = _group_rms_rows(branch(kr_ref, pk_ref, wk1_ref, wk2_ref), knc_ref[...])
    kc_ref[...] = jnp.where(valid, kc, 0.0).astype(BF16)
    vc = jnp.where(valid, branch(vr_ref, pv_ref, wv1_ref, wv2_ref), 0.0)
    vct_ref[...] = vc.T.astype(BF16)


def _compress_call(kr, vr, pk, pv, wk1, wv1, wk2, wv2, knc):
    b, n, w = kr.shape
    blk = pl.BlockSpec((None, n, w), lambda i: (i, 0, 0))
    consts = (pk, pv, wk1, wv1, wk2, wv2, knc)
    return pl.pallas_call(
        _compress_kernel,
        grid=(b,),
        in_specs=[blk, blk] + [_const_spec(c.shape) for c in consts],
        out_specs=[pl.BlockSpec((None, n, 128), lambda i: (i, 0, 0)),
                   pl.BlockSpec((None, 128, n), lambda i: (i, 0, 0))],
        out_shape=[jax.ShapeDtypeStruct((b, n, 128), BF16), jax.ShapeDtypeStruct((b, 128, n), BF16)],
        compiler_params=pltpu.CompilerParams(dimension_semantics=("arbitrary",), vmem_limit_bytes=VMEM_LIMIT),
        name="compress",
    )(kr, vr, *consts)


def _softmax_step(logits, k_unused, vt, m_ref, l_ref, acc_ref, h):
    del k_unused
    m_old = m_ref[h]
    m_new = jnp.maximum(m_old, jnp.max(logits, axis=0, keepdims=True))
    alpha = jnp.exp(m_old - m_new)
    p = jnp.exp(logits - m_new)
    l_ref[h] = alpha * l_ref[h] + jnp.sum(p, axis=0, keepdims=True)
    acc_ref[h] = alpha * acc_ref[h] + _dot(vt, p.astype(BF16))
    m_ref[h] = m_new


def _reset(m_ref, l_ref, acc_ref):
    m_ref[...] = jnp.full(m_ref.shape, NEG_INF, F32)
    l_ref[...] = jnp.zeros(l_ref.shape, F32)
    acc_ref[...] = jnp.zeros(acc_ref.shape, F32)


def _store_transposed(o_ref, head_out):
    for c in range(N_HEADS_A // 2):
        pair = jnp.concatenate([head_out(2 * c), head_out(2 * c + 1)], axis=0)
        o_ref[:, 128 * c:128 * (c + 1)] = pair.T


def _dsa_kernel(far_ref, kidx_ref, ka_ref, vat_ref, qi_ref, qa_ref, wt_ref, toe_ref, o_ref,
                keys_ref, m_ref, l_ref, acc_ref, *, k_top):
    qi = pl.program_id(1)
    nk = qi + 1
    q0 = qi * TQ
    t_pos = q0 + lax.broadcasted_iota(jnp.int32, (TS, TQ), 1)
    s_off = lax.broadcasted_iota(jnp.int32, (TS, TQ), 0)

    def score_body(kt, carry):
        s0 = pl.multiple_of(kt * TS, TS)
        kx = kidx_ref[pl.ds(s0, TS), :]
        acc = jnp.zeros((TS, TQ), F32)
        for h in range(N_IDX_HEADS):
            d = _dot(kx, qi_ref[IDX_DIM * h:IDX_DIM * (h + 1), :])
            acc = acc + wt_ref[h:h + 1, :] * jnp.maximum(d, 0.0)
        acc = jnp.where(acc == 0.0, 0.0, acc)
        bits = pltpu.bitcast(acc, jnp.int32)
        key = bits ^ ((bits >> 31) & jnp.int32(0x7FFFFFFF))
        keys_ref[pl.ds(s0, TS), :] = jnp.where(s0 + s_off <= t_pos, key, INT_MIN)
        return carry

    lax.fori_loop(0, nk, score_body, 0)

    def count(pred):
        def body(kt, c):
            s0 = pl.multiple_of(kt * TS, TS)
            hit = jnp.where(pred(keys_ref[pl.ds(s0, TS), :]), 1, 0)
            return c + jnp.sum(hit.reshape(TS // 8, 8, TQ), axis=0)
        c8 = lax.fori_loop(0, nk, body, jnp.zeros((8, TQ), jnp.int32))
        return jnp.sum(c8, axis=0, keepdims=True)

    def bisect(it, thr_u):
        cand_u = thr_u | lax.shift_left(jnp.int32(1), 31 - it)
        cand_s = cand_u ^ INT_MIN
        n_ge = count(lambda tile: tile >= cand_s)
        return jnp.where(n_ge >= k_top, cand_u, thr_u)

    thr_u = lax.fori_loop(0, 32, bisect, jnp.zeros((1, TQ), jnp.int32))
    thr = jnp.maximum(thr_u ^ INT_MIN, INT_MIN + 1)

    n_gt = count(lambda tile: tile > thr)
    n_ge = count(lambda tile: tile >= thr)
    tie = n_ge > k_top
    need = jnp.where(tie, (k_top - n_gt).astype(F32), float(2 ** 30))

    @pl.when(jnp.max(jnp.where(tie, 1, 0)) > 0)
    def _():
        lower = (lax.broadcasted_iota(jnp.int32, (TS, TS), 1)
                 <= lax.broadcasted_iota(jnp.int32, (TS, TS), 0))
        tri = jnp.where(lower, 1.0, 0.0).astype(BF16)

        def body(kt, run):
            s0 = pl.multiple_of(kt * TS, TS)
            tile = keys_ref[pl.ds(s0, TS), :]
            eq = tile == thr
            pref = _dot(tri, jnp.where(eq, 1.0, 0.0).astype(BF16))
            surplus = jnp.where(eq, run + pref, 0.0) > need
            keys_ref[pl.ds(s0, TS), :] = jnp.where(surplus, thr - 1, tile)
            return run + pref[TS - 1:TS, :]

        lax.fori_loop(0, nk, body, jnp.zeros((1, TQ), F32))

    _reset(m_ref, l_ref, acc_ref)

    def att_tile(kt, near):
        s0 = pl.multiple_of(kt * TS, TS)
        sel = keys_ref[pl.ds(s0, TS), :] >= thr
        k_t = ka_ref[pl.ds(s0, TS), :]
        for h in range(N_HEADS_A):
            g = h // (N_HEADS_A // N_KV_A)
            lg = _dot(k_t, qa_ref[128 * h:128 * (h + 1), :])
            lg = lg + (toe_ref[qi - kt, h] if near else far_ref[h])
            vt = vat_ref[HEAD_DIM * g:HEAD_DIM * (g + 1), pl.ds(s0, TS)]
            _softmax_step(jnp.where(sel, lg, NEG_INF), None, vt, m_ref, l_ref, acc_ref, h)

    n_far = jnp.maximum(qi - (FAR_TILES - 1), 0)
    lax.fori_loop(0, n_far, lambda kt, c: (att_tile(kt, False), c)[1], 0)
    lax.fori_loop(n_far, nk, lambda kt, c: (att_tile(kt, True), c)[1], 0)

    _store_transposed(o_ref, lambda h: acc_ref[h] / l_ref[h])


def _dsa_call(far, kidx, ka, vat, qi_t, qa_t, wt, toe):
    b, l, _ = ka.shape
    nq = l // TQ
    full_rows = lambda w: pl.BlockSpec((None, l, w), lambda i, j: (i, 0, 0))
    qcol = lambda r: pl.BlockSpec((None, r, TQ), lambda i, j: (i, 0, j))
    return pl.pallas_call(
        functools.partial(_dsa_kernel, k_top=min(TOPK_A, l // 4)),
        grid=(b, nq),
        in_specs=[pl.BlockSpec(memory_space=pltpu.SMEM),
                  full_rows(IDX_DIM), full_rows(128),
                  pl.BlockSpec((None, 128, l), lambda i, j: (i, 0, 0)),
                  qcol(N_IDX_HEADS * IDX_DIM), qcol(2 * D_A), qcol(N_IDX_HEADS),
                  _const_spec(toe.shape)],
        out_specs=pl.BlockSpec((None, TQ, D_A), lambda i, j: (i, j, 0)),
        out_shape=jax.ShapeDtypeStruct((b, l, D_A), F32),
        scratch_shapes=[pltpu.VMEM((l, TQ), jnp.int32),
                        pltpu.VMEM((N_HEADS_A, 1, TQ), F32),
                        pltpu.VMEM((N_HEADS_A, 1, TQ), F32),
                        pltpu.VMEM((N_HEADS_A, HEAD_DIM, TQ), F32)],
        compiler_params=pltpu.CompilerParams(dimension_semantics=("arbitrary", "arbitrary"),
                                             vmem_limit_bytes=VMEM_LIMIT),
        name="dsa",
    )(far, kidx, ka, vat, qi_t, qa_t, wt, toe)


def _nsa_kernel(far_ref, kc_ref, vct_ref, ks_ref, vst_ref, kw_ref, vwt_ref, qb_ref, gt_ref,
                toe_ref, toew_ref, cb_ref, slct_ref, o_ref,
                oc_ref, vals_ref, sel_ref, m_ref, l_ref, acc_ref, ow_ref, *, n_sel):
    qi = pl.program_id(1)
    nk = qi + 1
    q0 = qi * TQ
    n_c = kc_ref.shape[0]
    n_sb = slct_ref.shape[0]
    heads_per_group = N_HEADS_B // N_KV_B

    t_c = q0 + lax.broadcasted_iota(jnp.int32, (n_c, TQ), 1)
    end_c = lax.broadcasted_iota(jnp.int32, (n_c, TQ), 0) * CMP_STRIDE + (CMP_BLOCK - 1)
    mask_c = t_c >= end_c
    kc = kc_ref[...]
    slct = slct_ref[...]
    t_b = q0 + lax.broadcasted_iota(jnp.int32, (n_sb, TQ), 1)
    blk = lax.broadcasted_iota(jnp.int32, (n_sb, TQ), 0)
    jt = t_b >> int(math.log2(SLC_BLOCK))
    forced = (blk == 0) | (blk == jt) | (blk == jt - 1)
    for g in range(N_KV_B):
        imp = jnp.zeros((n_sb, TQ), F32)
        for r in range(heads_per_group):
            h = g * heads_per_group + r
            lg = _dot(kc, qb_ref[128 * h:128 * (h + 1), :]) + cb_ref[h]
            lg = jnp.where(mask_c, lg, NEG_INF)
            e = jnp.where(mask_c, jnp.exp(lg - jnp.max(lg, axis=0, keepdims=True)), 0.0)
            den = jnp.sum(e, axis=0, keepdims=True)
            p = e * jnp.where(den > 0.0, 1.0 / den, 0.0)
            p_hi = p.astype(BF16)
            p_lo = (p - p_hi.astype(F32)).astype(BF16)
            oc_ref[h] = _dot(vct_ref[HEAD_DIM * g:HEAD_DIM * (g + 1), :], p_hi)
            imp = imp + _dot(slct, p_hi) + _dot(slct, p_lo)
        vals = imp + jnp.where(forced, FORCE_BONUS, 0.0)
        vals = jnp.where(blk * SLC_BLOCK <= t_b, vals, NEG_INF)
        vals_ref[...] = vals
        rank = jnp.zeros((n_sb, TQ), jnp.int32)
        for m in range(n_sb):
            vm = vals_ref[m:m + 1, :]
            rank = rank + jnp.where(blk > m, jnp.where(vm >= vals, 1, 0), jnp.where(vm > vals, 1, 0))
        sel_ref[g] = jnp.where(rank < n_sel, 1.0, 0.0)

    _reset(m_ref, l_ref, acc_ref)
    blocks_per_tile = TS // SLC_BLOCK

    def sel_tile(kt, near):
        s0 = pl.multiple_of(kt * TS, TS)
        k_t = ks_ref[pl.ds(s0, TS), :]
        for g in range(N_KV_B):
            rows = [jnp.broadcast_to(sel_ref[g, pl.ds(kt * blocks_per_tile + i, 1), :], (SLC_BLOCK, TQ))
                    for i in range(blocks_per_tile)]
            sel = jnp.concatenate(rows, axis=0) > 0.5
            vt = vst_ref[HEAD_DIM * g:HEAD_DIM * (g + 1), pl.ds(s0, TS)]
            for r in range(heads_per_group):
                h = g * heads_per_group + r
                lg = _dot(k_t, qb_ref[128 * h:128 * (h + 1), :])
                lg = lg + (toe_ref[qi - kt, h] if near else far_ref[h])
                _softmax_step(jnp.where(sel, lg, NEG_INF), None, vt, m_ref, l_ref, acc_ref, h)

    n_far = jnp.maximum(qi - (FAR_TILES - 1), 0)
    lax.fori_loop(0, n_far, lambda kt, c: (sel_tile(kt, False), c)[1], 0)
    lax.fori_loop(n_far, nk, lambda kt, c: (sel_tile(kt, True), c)[1], 0)
    for h in range(N_HEADS_B):
        ow_ref[h] = gt_ref[3 * h:3 * h + 1, :] * oc_ref[h] + gt_ref[3 * h + 1:3 * h + 2, :] * (acc_ref[h] / l_ref[h])

    _reset(m_ref, l_ref, acc_ref)

    def win_tile(kt, c):
        s0 = pl.multiple_of(kt * TS, TS)
        k_t = kw_ref[pl.ds(s0, TS), :]
        for h in range(N_HEADS_B):
            g = h // heads_per_group
            lg = _dot(k_t, qb_ref[128 * h:128 * (h + 1), :]) + toew_ref[qi - kt, h]
            vt = vwt_ref[HEAD_DIM * g:HEAD_DIM * (g + 1), pl.ds(s0, TS)]
            _softmax_step(lg, None, vt, m_ref, l_ref, acc_ref, h)
        return c

    lax.fori_loop(jnp.maximum(qi - WINDOW // TS, 0), nk, win_tile, 0)

    _store_transposed(o_ref, lambda h: ow_ref[h] + gt_ref[3 * h + 2:3 * h + 3, :] * (acc_ref[h] / l_ref[h]))


def _nsa_call(far, kc, vct, ks, vst, kw, vwt, qb_t, gt, toe, toew, cb, slct):
    b, l, _ = ks.shape
    nq = l // TQ
    n_c = kc.shape[1]
    n_sb = slct.shape[0]
    full_rows = pl.BlockSpec((None, l, 128), lambda i, j: (i, 0, 0))
    full_cols = pl.BlockSpec((None, 128, l), lambda i, j: (i, 0, 0))
    qcol = lambda r: pl.BlockSpec((None, r, TQ), lambda i, j: (i, 0, j))
    head_buf = lambda rows: pltpu.VMEM((N_HEADS_B, rows, TQ), F32)
    return pl.pallas_call(
        functools.partial(_nsa_kernel, n_sel=min(N_SLC, n_sb)),
        grid=(b, nq),
        in_specs=[pl.BlockSpec(memory_space=pltpu.SMEM),
                  pl.BlockSpec((None, n_c, 128), lambda i, j: (i, 0, 0)),
                  pl.BlockSpec((None, 128, n_c), lambda i, j: (i, 0, 0)),
                  full_rows, full_cols, full_rows, full_cols,
                  qcol(2 * D_B), qcol(N_HEADS_B * 3),
                  _const_spec(toe.shape), _const_spec(toew.shape),
                  pl.BlockSpec((None, N_HEADS_B, n_c, TQ), lambda i, j: (j, 0, 0, 0)),
                  _const_spec(slct.shape)],
        out_specs=pl.BlockSpec((None, TQ, D_B), lambda i, j: (i, j, 0)),
        out_shape=jax.ShapeDtypeStruct((b, l, D_B), F32),
        scratch_shapes=[head_buf(HEAD_DIM),
                        pltpu.VMEM((n_sb, TQ), F32),
                        pltpu.VMEM((N_KV_B, n_sb, TQ), F32),
                        head_buf(1), head_buf(1), head_buf(HEAD_DIM), head_buf(HEAD_DIM)],
        compiler_params=pltpu.CompilerParams(dimension_semantics=("arbitrary", "arbitrary"),
                                             vmem_limit_bytes=VMEM_LIMIT),
        name="nsa",
    )(far, kc, vct, ks, vst, kw, vwt, qb_t, gt, toe, toew, cb, slct)


def _out_kernel(x_ref, oa_ref, ob_ref, sz_ref, wa_ref, wb_ref, y_ref):
    sz = sz_ref[...]
    ya = (oa_ref[...] * sz[:, :D_A]).astype(BF16)
    yb = (ob_ref[...] * sz[:, D_A:]).astype(BF16)
    y_ref[...] = x_ref[...] + _dot(ya, wa_ref[...]) + _dot(yb, wb_ref[...])


def _out_call(x, oa, ob, sz, wa, wb):
    b, l, d = x.shape
    t = min(T_OUT, l)
    row = lambda w: pl.BlockSpec((None, t, w), lambda i, j: (i, j, 0))
    return pl.pallas_call(
        _out_kernel,
        grid=(b, l // t),
        in_specs=[row(d), row(D_A), row(D_B), row(D_A + D_B), _const_spec(wa.shape), _const_spec(wb.shape)],
        out_specs=row(d),
        out_shape=jax.ShapeDtypeStruct((b, l, d), F32),
        compiler_params=pltpu.CompilerParams(dimension_semantics=("arbitrary", "arbitrary"),
                                             vmem_limit_bytes=VMEM_LIMIT),
        name="out",
    )(x, oa, ob, sz, wa, wb)


def _t5_bucket(dist):
    n = jnp.maximum(dist, 0)
    max_exact = N_BUCKETS // 2
    nf = jnp.maximum(n, 1).astype(jnp.float32)
    log_b = max_exact + (jnp.log(nf / max_exact) / math.log(MAX_DISTANCE / max_exact)
                         * (N_BUCKETS - max_exact)).astype(jnp.int32)
    return jnp.where(n < max_exact, n, jnp.minimum(log_b, N_BUCKETS - 1))


def _bias_tables(rel_bias, l):
    n_tab = max(l, (FAR_TILES + 1) * TQ)
    tab = rel_bias[_t5_bucket(jnp.arange(n_tab))].T.astype(F32)

    def lookup(dist):
        vals = jnp.take(tab, jnp.clip(dist, 0, n_tab - 1), axis=1)
        return jnp.where(dist >= 0, vals, NEG_INF)

    i = np.arange(TS)[None, :, None]
    j = np.arange(TQ)[None, None, :]
    d = np.arange(FAR_TILES)[:, None, None]
    dist = jnp.asarray(TQ * d + j - i)
    toe = jnp.swapaxes(lookup(dist), 0, 1)
    toe_a, toe_b = toe[:, :N_HEADS_A], toe[:, N_HEADS_A:]
    n_win = WINDOW // TS + 1
    toe_w = jnp.where((dist < WINDOW)[:n_win, None], toe_b[:n_win], NEG_INF)
    n_c = l // CMP_STRIDE
    nq = l // TQ
    c = np.arange(n_c)[None, :, None]
    q = np.arange(nq)[:, None, None]
    dist_c = jnp.asarray(TQ * q + j - (CMP_STRIDE * c + CMP_BLOCK - 1))
    cb = jnp.swapaxes(lookup(dist_c)[N_HEADS_A:], 0, 1)
    far = rel_bias[N_BUCKETS - 1].astype(F32)
    return toe_a, toe_b, toe_w, cb, far[:N_HEADS_A], far[N_HEADS_A:]


def _overlap_map_t(n_c, n_sb):
    c_start = np.arange(n_c)[None, :] * CMP_STRIDE
    b_start = np.arange(n_sb)[:, None] * SLC_BLOCK
    ov = np.minimum(c_start + CMP_BLOCK, b_start + SLC_BLOCK) - np.maximum(c_start, b_start)
    return (np.maximum(ov, 0) / CMP_BLOCK).astype(np.float32)


def _pick(w, names, pad_to=None):
    parts = [w[:, PROJ_COLS[n][0]:PROJ_COLS[n][1]] for n in names]
    out = jnp.concatenate(parts, axis=1)
    if pad_to is not None and out.shape[1] < pad_to:
        out = jnp.pad(out, ((0, 0), (0, pad_to - out.shape[1])))
    return out


def _compress_weights(w1, w2, pos):
    half = CMP_BLOCK // 2
    eye = jnp.eye(N_KV_B, dtype=w1.dtype)
    w1r = w1.reshape(2, half, HEAD_DIM, HEAD_DIM)
    w1b = jnp.einsum("sjde,gk->sjgdke", w1r, eye).reshape(2, half * N_KV_B * HEAD_DIM, N_KV_B * HEAD_DIM)
    w2b = jnp.einsum("de,gk->gdke", w2, eye).reshape(N_KV_B * HEAD_DIM, N_KV_B * HEAD_DIM)
    posr = jnp.broadcast_to(pos.reshape(2, half, 1, HEAD_DIM), (2, half, N_KV_B, HEAD_DIM)).reshape(2, -1)
    return w1b.astype(BF16), w2b.astype(BF16), posr.astype(F32)


def kernel(x, rel_bias, norm_g, w_in, b_gate_b, kv_norm_a, w_kv_up_a, q_norm_a, k_norm_a, q_norm_b, k_norm_cmp,
           k_norm_slc, k_norm_win, cmp_pos_k, cmp_pos_v, w_cmp_k1, w_cmp_k2, w_cmp_v1, w_cmp_v2, w_out):
    b, l, d = x.shape
    depth = norm_g.shape[0]
    assert d == D_MODEL and l % T_PROJ == 0 and l % TQ == 0 and l >= (FAR_TILES + 1) * TQ
    n_c = l // CMP_STRIDE
    n_sb = l // SLC_BLOCK
    toe_a, toe_b, toe_w, cb, far_a, far_b = _bias_tables(rel_bias, l)
    slct = jnp.asarray(_overlap_map_t(n_c, n_sb)).astype(BF16)
    slct = slct * jnp.asarray(np.arange(n_c)[None, :] < n_c - 1, BF16)
    two = lambda v: jnp.tile(v, N_KV_B)[None, :].astype(F32)
    colv = lambda v: v[:, None].astype(F32)
    for layer in range(depth):
        w = w_in[layer]
        wrow = _pick(w, ROW_ORDER, ROW_WIDTH).astype(BF16)
        wcol = _pick(w, COL_ORDER).T.astype(BF16)
        wup = w_kv_up_a[layer]
        n_k = N_KV_A * HEAD_DIM
        (ka, kidx, ks, kw, kcmp, vcmp, sz, qa_t, qi_t, qb_t, va_t, vs_t, vw_t, g_t, w_t) = _proj_call(
            x, norm_g[layer][None, :], wrow, wcol, kv_norm_a[layer][None, :],
            wup[:, :n_k].astype(BF16), wup[:, n_k:].T.astype(BF16),
            two(k_norm_a[layer]), two(k_norm_slc[layer]), two(k_norm_win[layer]),
            colv(q_norm_a[layer]), colv(q_norm_b[layer]), colv(b_gate_b[layer]))
        wk1, wk2, pk = _compress_weights(w_cmp_k1[layer], w_cmp_k2[layer], cmp_pos_k[layer])
        wv1, wv2, pv = _compress_weights(w_cmp_v1[layer], w_cmp_v2[layer], cmp_pos_v[layer])
        row16 = CMP_STRIDE * N_KV_B * HEAD_DIM
        kc, vc_t = _compress_call(kcmp.reshape(b, n_c, row16), vcmp.reshape(b, n_c, row16),
                                  pk, pv, wk1, wv1, wk2, wv2, two(k_norm_cmp[layer]))
        o_a = _dsa_call(far_a, kidx, ka, va_t, qi_t, qa_t, w_t, toe_a)
        o_b = _nsa_call(far_b, kc, vc_t, ks, vs_t, kw, vw_t, qb_t, g_t, toe_b, toe_w, cb, slct)
        wo = w_out[layer].astype(BF16)
        x = _out_call(x, o_a, o_b, sz, wo[:D_A], wo[D_A:])
    return x
```

```python
import functools
import math

import jax
import jax.numpy as jnp
import numpy as np
from jax import lax
from jax.experimental import pallas as pl
from jax.experimental.pallas import tpu as pltpu

D_MODEL = 1024
HEAD_DIM = 64
N_HEADS_A = 8
N_HEADS_B = 8
D_A = N_HEADS_A * HEAD_DIM
D_B = N_HEADS_B * HEAD_DIM
KV_RANK_A = 128
N_KV_A = 2
N_IDX_HEADS = 8
IDX_DIM = 64
TOPK_A = 256
N_KV_B = 2
D_KV_B = N_KV_B * HEAD_DIM
CMP_BLOCK = 32
CMP_STRIDE = 16
SLC_BLOCK = 64
N_SLC = 16
WINDOW = 512
N_BUCKETS = 32
MAX_DISTANCE = 1024
EPS = 1e-6
NEG_INF = -1e30
FORCE_BONUS = 1e4
ATTN_SCALE = HEAD_DIM ** -0.5
IDX_SCALE = IDX_DIM ** -0.5
IDX_W_SCALE = N_IDX_HEADS ** -0.5

PROJ_SIZES = (
    D_A, KV_RANK_A, N_IDX_HEADS * IDX_DIM, IDX_DIM, N_IDX_HEADS, D_A,
    D_B, D_KV_B, D_KV_B, D_KV_B, D_KV_B, D_KV_B, D_KV_B, N_HEADS_B * 3, D_B,
)
PROJ_NAMES = ("q_a", "c_kv", "q_idx", "k_idx", "w_idx", "z_a",
              "q_b", "k_cmp", "v_cmp", "k_slc", "v_slc", "k_win", "v_win", "g_b", "z_b")
_OFFS = np.concatenate([[0], np.cumsum(PROJ_SIZES)])
PROJ_COLS = {n: (int(_OFFS[i]), int(_OFFS[i + 1])) for i, n in enumerate(PROJ_NAMES)}

TQ = 256
TS = 256
T_PROJ = 512
T_OUT = 512
FAR_DIST = MAX_DISTANCE + 1
FAR_TILES = -(-(FAR_DIST + TS - 1) // TQ)
TOE_WIDTH = FAR_TILES * TQ
TOE_PERIOD = TOE_WIDTH + TS
CB_STEP = TQ // CMP_STRIDE
CB_BACK = -(-(FAR_DIST + CMP_BLOCK - 1 - CMP_STRIDE) // (CMP_STRIDE * CB_STEP)) * CB_STEP
CB_NEAR = CB_BACK + CB_STEP
CB_ROWS = CB_BACK + CB_NEAR
CB_OFFSET = CMP_STRIDE * CB_BACK - (CMP_BLOCK - 1)
VMEM_LIMIT = 56 * 1024 * 1024
INT_MIN = -(2 ** 31)

F32 = jnp.float32
BF16 = jnp.bfloat16
NT_DIMS = (((1,), (1,)), ((), ()))

ROW_ORDER = ("c_kv", "z_a", "z_b", "k_cmp", "v_cmp", "k_slc", "k_win", "k_idx")
ROW_WIDTH = 1792
COL_ORDER = ("q_a", "q_idx", "q_b", "v_slc", "v_win", "g_b", "w_idx")


def _layout(order):
    offs, o = {}, 0
    for n in order:
        w = PROJ_COLS[n][1] - PROJ_COLS[n][0]
        offs[n] = (o, o + w)
        o += w
    return offs, o


ROW_OFFS, _ROW_USED = _layout(ROW_ORDER)
COL_OFFS, COL_ROWS = _layout(COL_ORDER)


def _dot(a, b):
    return jnp.dot(a, b, preferred_element_type=F32)


def _dot_nt(a, b):
    return lax.dot_general(a, b, NT_DIMS, preferred_element_type=F32)


def _rms_rows(a, gain):
    ms = jnp.mean(a * a, axis=-1, keepdims=True)
    return a * lax.rsqrt(ms + EPS) * gain


def _group_rms_rows(a, gain):
    lane = lax.broadcasted_iota(jnp.int32, a.shape, 1)
    sq = a * a
    lo = lane < HEAD_DIM
    s_lo = jnp.sum(jnp.where(lo, sq, 0.0), axis=-1, keepdims=True)
    s_hi = jnp.sum(jnp.where(lo, 0.0, sq), axis=-1, keepdims=True)
    ms = jnp.where(lo, s_lo, s_hi) * (1.0 / HEAD_DIM)
    return a * lax.rsqrt(ms + EPS) * gain


def _proj_kernel(x_ref, g_ref, wrow_ref, wcol_ref, kvn_ref, wup_ref, wupv_ref, kna_ref, kns_ref, knw_ref,
                 qna_ref, qnb_ref, bg_ref,
                 ka_ref, kidx_ref, ks_ref, kw_ref, kcmp_ref, vcmp_ref, sz_ref,
                 qa_ref, qi_ref, qb_ref, va_ref, vs_ref, vw_ref, gt_ref, wt_ref):
    t = x_ref.shape[0]
    h = _rms_rows(x_ref[...], g_ref[...]).astype(BF16)
    pr = _dot(h, wrow_ref[...])

    def rows(name):
        a, b = ROW_OFFS[name]
        return pr[:, a:b]

    ckvn = _rms_rows(rows("c_kv"), kvn_ref[...]).astype(BF16)
    k_up = _dot(ckvn, wup_ref[...])
    ka_ref[...] = _group_rms_rows(k_up, kna_ref[...]).astype(BF16)
    va_ref[...] = _dot_nt(wupv_ref[...], ckvn).astype(BF16)
    za, zb = ROW_OFFS["z_a"][0], ROW_OFFS["z_b"][1]
    z = pr[:, za:zb]
    sz_ref[...] = z * jax.nn.sigmoid(z)
    kcmp_ref[...] = rows("k_cmp")
    vcmp_ref[...] = rows("v_cmp")
    ks_ref[...] = _group_rms_rows(rows("k_slc"), kns_ref[...]).astype(BF16)
    kw_ref[...] = _group_rms_rows(rows("k_win"), knw_ref[...]).astype(BF16)
    kidx_ref[...] = rows("k_idx").astype(BF16)

    pt = _dot_nt(wcol_ref[...], h)

    def cols(name):
        a, b = COL_OFFS[name]
        return pt[a:b, :]

    def qnorm_padded(q, gain_col):
        q = q.reshape(N_HEADS_A, HEAD_DIM, t)
        ms = jnp.mean(q * q, axis=1, keepdims=True)
        qn = (q * lax.rsqrt(ms + EPS) * gain_col[None] * ATTN_SCALE).astype(BF16)
        zeros = jnp.zeros((HEAD_DIM, t), BF16)
        parts = []
        for hh in range(N_HEADS_A):
            parts += [qn[hh], zeros] if hh < N_HEADS_A // N_KV_A else [zeros, qn[hh]]
        return jnp.concatenate(parts, axis=0)

    qa_ref[...] = qnorm_padded(cols("q_a"), qna_ref[...])
    qb_ref[...] = qnorm_padded(cols("q_b"), qnb_ref[...])
    qi_ref[...] = (cols("q_idx") * IDX_SCALE).astype(BF16)
    vs_ref[...] = cols("v_slc").astype(BF16)
    vw_ref[...] = cols("v_win").astype(BF16)
    gt_ref[...] = jax.nn.sigmoid(cols("g_b") + bg_ref[...])
    wt_ref[...] = cols("w_idx") * IDX_W_SCALE


def _const_spec(shape):
    nd = len(shape)
    return pl.BlockSpec(shape, lambda *_: (0,) * nd)


def _proj_call(x, g, wrow, wcol, kvn, wup, wupv, kna, kns, knw, qna, qnb, bg):
    b, l, _ = x.shape
    t = min(T_PROJ, l)
    grid = (b, l // t)
    row = lambda w: pl.BlockSpec((None, t, w), lambda i, j: (i, j, 0))
    col = lambda r: pl.BlockSpec((None, r, t), lambda i, j: (i, 0, j))
    rs = lambda w, dt: jax.ShapeDtypeStruct((b, l, w), dt)
    cs = lambda r, dt: jax.ShapeDtypeStruct((b, r, l), dt)
    consts = (g, wrow, wcol, kvn, wup, wupv, kna, kns, knw, qna, qnb, bg)
    return pl.pallas_call(
        _proj_kernel,
        grid=grid,
        in_specs=[row(D_MODEL)] + [_const_spec(c.shape) for c in consts],
        out_specs=[row(128), row(IDX_DIM), row(128), row(128), row(128), row(128), row(D_A + D_B),
                   col(2 * D_A), col(N_IDX_HEADS * IDX_DIM), col(2 * D_B), col(128), col(128), col(128),
                   col(N_HEADS_B * 3), col(N_IDX_HEADS)],
        out_shape=[rs(128, BF16), rs(IDX_DIM, BF16), rs(128, BF16), rs(128, BF16), rs(128, F32), rs(128, F32),
                   rs(D_A + D_B, F32),
                   cs(2 * D_A, BF16), cs(N_IDX_HEADS * IDX_DIM, BF16), cs(2 * D_B, BF16), cs(128, BF16),
                   cs(128, BF16), cs(128, BF16), cs(N_HEADS_B * 3, F32), cs(N_IDX_HEADS, F32)],
        compiler_params=pltpu.CompilerParams(dimension_semantics=("arbitrary", "arbitrary"),
                                             vmem_limit_bytes=VMEM_LIMIT),
        name="proj",
    )(x, *consts)


def _compress_kernel(kr_ref, vr_ref, pk_ref, pv_ref, wk1_ref, wv1_ref, wk2_ref, wv2_ref, knc_ref,
                     kc_ref, vct_ref):
    n = kr_ref.shape[0]
    row = lax.broadcasted_iota(jnp.int32, (n, 128), 0)
    valid = row < n - 1

    def branch(r_ref, p_ref, w1_ref, w2_ref):
        r = r_ref[...]
        a = _dot((r + p_ref[0:1, :]).astype(BF16), w1_ref[0])
        b = _dot((r + p_ref[1:2, :]).astype(BF16), w1_ref[1])
        pre = a + pltpu.roll(b, n - 1, axis=0)
        return _dot(jax.nn.gelu(pre).astype(BF16), w2_ref[...])

    kc = _group_rms_rows(branch(kr_ref, pk_ref, wk1_ref, wk2_ref), knc_ref[...])
    kc_ref[...] = jnp.where(valid, kc, 0.0).astype(BF16)
    vc = jnp.where(valid, branch(vr_ref, pv_ref, wv1_ref, wv2_ref), 0.0)
    vct_ref[...] = vc.T.astype(BF16)


def _compress_call(kr, vr, pk, pv, wk1, wv1, wk2, wv2, knc):
    b, n, w = kr.shape
    blk = pl.BlockSpec((None, n, w), lambda i: (i, 0, 0))
    consts = (pk, pv, wk1, wv1, wk2, wv2, knc)
    return pl.pallas_call(
        _compress_kernel,
        grid=(b,),
        in_specs=[blk, blk] + [_const_spec(c.shape) for c in consts],
        out_specs=[pl.BlockSpec((None, n, 128), lambda i: (i, 0, 0)),
                   pl.BlockSpec((None, 128, n), lambda i: (i, 0, 0))],
        out_shape=[jax.ShapeDtypeStruct((b, n, 128), BF16), jax.ShapeDtypeStruct((b, 128, n), BF16)],
        compiler_params=pltpu.CompilerParams(dimension_semantics=("arbitrary",), vmem_limit_bytes=VMEM_LIMIT),
        name="compress",
    )(kr, vr, *consts)


def _attend_tile(k_t, q_ref, vt_of, bias_of, shift_of, lg_ref, p_ref, m_ref, l_ref, acc_ref):
    heads = range(N_HEADS_A)
    for h in heads:
        lg_ref[h] = _dot(k_t, q_ref[128 * h:128 * (h + 1), :])
    alphas = []
    for h in heads:
        lg = lg_ref[h]
        for term in bias_of(h):
            lg = lg + term
        shift = shift_of(h)
        m_old = m_ref[h]
        m_new = jnp.maximum(m_old, jnp.max(lg, axis=0, keepdims=True) + shift)
        alpha = jnp.exp(m_old - m_new)
        p = jnp.exp(lg - (m_new - shift))
        l_ref[h] = alpha * l_ref[h] + jnp.sum(p, axis=0, keepdims=True)
        m_ref[h] = m_new
        p_ref[h] = p.astype(BF16)
        alphas.append(alpha)
    for h in heads:
        acc_ref[h] = alphas[h] * acc_ref[h] + _dot(vt_of(h), p_ref[h])


def _build_toeplitz(tab_ref, toe_ref):
    period = tab_ref.shape[-1]
    for h in range(toe_ref.shape[0]):
        rows = jnp.broadcast_to(tab_ref[h], (TS, period))
        toe_ref[h] = pltpu.roll(rows, 0, 1, stride=1, stride_axis=0)[:, :toe_ref.shape[-1]]


def _first_step():
    return jnp.logical_and(pl.program_id(0) == 0, pl.program_id(1) == 0)


def _reset(m_ref, l_ref, acc_ref):
    m_ref[...] = jnp.full(m_ref.shape, NEG_INF, F32)
    l_ref[...] = jnp.zeros(l_ref.shape, F32)
    acc_ref[...] = jnp.zeros(acc_ref.shape, F32)


def _store_transposed(o_ref, head_out):
    for c in range(N_HEADS_A // 2):
        pair = jnp.concatenate([head_out(2 * c), head_out(2 * c + 1)], axis=0)
        o_ref[:, 128 * c:128 * (c + 1)] = pair.T


def _dsa_kernel(far_ref, kidx_ref, ka_ref, vat_ref, qi_ref, qa_ref, wt_ref, tab_ref, o_ref,
                keys_ref, toe_ref, lg_ref, p_ref, m_ref, l_ref, acc_ref, *, k_top):
    @pl.when(_first_step())
    def _():
        _build_toeplitz(tab_ref, toe_ref)

    qi = pl.program_id(1)
    nk = qi + 1
    q0 = qi * TQ
    t_pos = q0 + lax.broadcasted_iota(jnp.int32, (TS, TQ), 1)
    s_off = lax.broadcasted_iota(jnp.int32, (TS, TQ), 0)

    def score_body(kt, carry):
        s0 = pl.multiple_of(kt * TS, TS)
        kx = kidx_ref[pl.ds(s0, TS), :]
        acc = jnp.zeros((TS, TQ), F32)
        for h in range(N_IDX_HEADS):
            d = _dot(kx, qi_ref[IDX_DIM * h:IDX_DIM * (h + 1), :])
            acc = acc + wt_ref[h:h + 1, :] * jnp.maximum(d, 0.0)
        acc = jnp.where(acc == 0.0, 0.0, acc)
        bits = pltpu.bitcast(acc, jnp.int32)
        key = bits ^ ((bits >> 31) & jnp.int32(0x7FFFFFFF))
        keys_ref[pl.ds(s0, TS), :] = jnp.where(s0 + s_off <= t_pos, key, INT_MIN)
        return carry

    lax.fori_loop(0, nk, score_body, 0)

    def count(pred):
        def body(kt, c):
            s0 = pl.multiple_of(kt * TS, TS)
            hit = jnp.where(pred(keys_ref[pl.ds(s0, TS), :]), 1, 0)
            return c + jnp.sum(hit.reshape(TS // 8, 8, TQ), axis=0)
        c8 = lax.fori_loop(0, nk, body, jnp.zeros((8, TQ), jnp.int32))
        return jnp.sum(c8, axis=0, keepdims=True)

    def bisect(it, thr_u):
        cand_u = thr_u | lax.shift_left(jnp.int32(1), 31 - it)
        cand_s = cand_u ^ INT_MIN
        n_ge = count(lambda tile: tile >= cand_s)
        return jnp.where(n_ge >= k_top, cand_u, thr_u)

    thr_u = lax.fori_loop(0, 32, bisect, jnp.zeros((1, TQ), jnp.int32))
    thr = jnp.maximum(thr_u ^ INT_MIN, INT_MIN + 1)

    n_gt = count(lambda tile: tile > thr)
    n_ge = count(lambda tile: tile >= thr)
    tie = n_ge > k_top
    need = jnp.where(tie, (k_top - n_gt).astype(F32), float(2 ** 30))

    @pl.when(jnp.max(jnp.where(tie, 1, 0)) > 0)
    def _():
        lower = (lax.broadcasted_iota(jnp.int32, (TS, TS), 1)
                 <= lax.broadcasted_iota(jnp.int32, (TS, TS), 0))
        tri = jnp.where(lower, 1.0, 0.0).astype(BF16)

        def body(kt, run):
            s0 = pl.multiple_of(kt * TS, TS)
            tile = keys_ref[pl.ds(s0, TS), :]
            eq = tile == thr
            pref = _dot(tri, jnp.where(eq, 1.0, 0.0).astype(BF16))
            surplus = jnp.where(eq, run + pref, 0.0) > need
            keys_ref[pl.ds(s0, TS), :] = jnp.where(surplus, thr - 1, tile)
            return run + pref[TS - 1:TS, :]

        lax.fori_loop(0, nk, body, jnp.zeros((1, TQ), F32))

    _reset(m_ref, l_ref, acc_ref)

    def att_tile(kt, near):
        s0 = pl.multiple_of(kt * TS, TS)
        off = pl.multiple_of((qi - kt) * TQ, TQ)
        unselected = jnp.where(keys_ref[pl.ds(s0, TS), :] >= thr, 0.0, NEG_INF)

        def vt_of(h):
            g = h // (N_HEADS_A // N_KV_A)
            return vat_ref[HEAD_DIM * g:HEAD_DIM * (g + 1), pl.ds(s0, TS)]

        if near:
            bias_of = lambda h: (toe_ref[h, :, pl.ds(off, TQ)], unselected)
            shift_of = lambda h: 0.0
        else:
            bias_of = lambda h: (unselected,)
            shift_of = lambda h: far_ref[h]
        _attend_tile(ka_ref[pl.ds(s0, TS), :], qa_ref, vt_of, bias_of, shift_of,
                     lg_ref, p_ref, m_ref, l_ref, acc_ref)

    n_far = jnp.maximum(qi - (FAR_TILES - 1), 0)
    lax.fori_loop(0, n_far, lambda kt, c: (att_tile(kt, False), c)[1], 0)
    lax.fori_loop(n_far, nk, lambda kt, c: (att_tile(kt, True), c)[1], 0)

    _store_transposed(o_ref, lambda h: acc_ref[h] / l_ref[h])


def _attn_scratch():
    return [pltpu.VMEM((N_HEADS_A, TS, TQ), F32),
            pltpu.VMEM((N_HEADS_A, TS, TQ), BF16),
            pltpu.VMEM((N_HEADS_A, 1, TQ), F32),
            pltpu.VMEM((N_HEADS_A, 1, TQ), F32),
            pltpu.VMEM((N_HEADS_A, HEAD_DIM, TQ), F32)]


def _dsa_call(far, kidx, ka, vat, qi_t, qa_t, wt, tab):
    b, l, _ = ka.shape
    nq = l // TQ
    full_rows = lambda w: pl.BlockSpec((None, l, w), lambda i, j: (i, 0, 0))
    qcol = lambda r: pl.BlockSpec((None, r, TQ), lambda i, j: (i, 0, j))
    return pl.pallas_call(
        functools.partial(_dsa_kernel, k_top=min(TOPK_A, l // 4)),
        grid=(b, nq),
        in_specs=[pl.BlockSpec(memory_space=pltpu.SMEM),
                  full_rows(IDX_DIM), full_rows(128),
                  pl.BlockSpec((None, 128, l), lambda i, j: (i, 0, 0)),
                  qcol(N_IDX_HEADS * IDX_DIM), qcol(2 * D_A), qcol(N_IDX_HEADS),
                  _const_spec(tab.shape)],
        out_specs=pl.BlockSpec((None, TQ, D_A), lambda i, j: (i, j, 0)),
        out_shape=jax.ShapeDtypeStruct((b, l, D_A), F32),
        scratch_shapes=[pltpu.VMEM((l, TQ), jnp.int32),
                        pltpu.VMEM((N_HEADS_A, TS, TOE_WIDTH), F32)] + _attn_scratch(),
        compiler_params=pltpu.CompilerParams(dimension_semantics=("arbitrary", "arbitrary"),
                                             vmem_limit_bytes=VMEM_LIMIT),
        name="dsa",
    )(far, kidx, ka, vat, qi_t, qa_t, wt, tab)


def _nsa_kernel(far_ref, kc_ref, vct_ref, ks_ref, vst_ref, kw_ref, vwt_ref, qb_ref, gt_ref,
                tab_ref, tabc_ref, slct_ref, o_ref,
                toe_ref, toew_ref, cbn_ref, lgc_ref, oc_ref, vals_ref, sel_ref, ow_ref,
                lg_ref, p_ref, m_ref, l_ref, acc_ref, *, n_sel):
    n_win = WINDOW // TS

    @pl.when(_first_step())
    def _():
        _build_toeplitz(tab_ref, toe_ref)
        inside = (lax.broadcasted_iota(jnp.int32, (TS, TQ), 1) < lax.broadcasted_iota(jnp.int32, (TS, TQ), 0))
        for h in range(N_HEADS_B):
            toew_ref[h] = jnp.where(inside, toe_ref[h, :, n_win * TQ:(n_win + 1) * TQ], NEG_INF)
            rows = jnp.broadcast_to(tabc_ref[h], (CB_ROWS, TOE_PERIOD))
            cbn_ref[h] = pltpu.roll(rows, 0, 1, stride=CMP_STRIDE, stride_axis=0)[:, :TQ]

    qi = pl.program_id(1)
    nk = qi + 1
    q0 = qi * TQ
    n_c = kc_ref.shape[0]
    n_sb = slct_ref.shape[0]
    heads_per_group = N_HEADS_B // N_KV_B
    cb_row0 = pl.multiple_of(jnp.maximum(qi * CB_STEP - CB_BACK, 0), CB_STEP)
    cb_tab0 = pl.multiple_of(CB_BACK - (qi * CB_STEP - cb_row0), CB_STEP)

    t_c = q0 + lax.broadcasted_iota(jnp.int32, (n_c, TQ), 1)
    end_c = lax.broadcasted_iota(jnp.int32, (n_c, TQ), 0) * CMP_STRIDE + (CMP_BLOCK - 1)
    mask_c = t_c >= end_c
    kc = kc_ref[...]
    slct = slct_ref[...]
    t_b = q0 + lax.broadcasted_iota(jnp.int32, (n_sb, TQ), 1)
    blk = lax.broadcasted_iota(jnp.int32, (n_sb, TQ), 0)
    jt = t_b >> int(math.log2(SLC_BLOCK))
    forced = (blk == 0) | (blk == jt) | (blk == jt - 1)
    for g in range(N_KV_B):
        imp = jnp.zeros((n_sb, TQ), F32)
        for r in range(heads_per_group):
            h = g * heads_per_group + r
            lgc_ref[...] = _dot(kc, qb_ref[128 * h:128 * (h + 1), :]) + far_ref[h]
            lgc_ref[pl.ds(cb_row0, CB_NEAR), :] = (lgc_ref[pl.ds(cb_row0, CB_NEAR), :]
                                                   + cbn_ref[h, pl.ds(cb_tab0, CB_NEAR), :])
            lg = jnp.where(mask_c, lgc_ref[...], NEG_INF)
            e = jnp.where(mask_c, jnp.exp(lg - jnp.max(lg, axis=0, keepdims=True)), 0.0)
            den = jnp.sum(e, axis=0, keepdims=True)
            p = e * jnp.where(den > 0.0, 1.0 / den, 0.0)
            p_hi = p.astype(BF16)
            p_lo = (p - p_hi.astype(F32)).astype(BF16)
            oc_ref[h] = _dot(vct_ref[HEAD_DIM * g:HEAD_DIM * (g + 1), :], p_hi)
            imp = imp + _dot(slct, p_hi) + _dot(slct, p_lo)
        vals = imp + jnp.where(forced, FORCE_BONUS, 0.0)
        vals = jnp.where(blk * SLC_BLOCK <= t_b, vals, NEG_INF)
        vals_ref[...] = vals
        rank = jnp.zeros((n_sb, TQ), jnp.int32)
        for m in range(n_sb):
            vm = vals_ref[m:m + 1, :]
            rank = rank + jnp.where(blk > m, jnp.where(vm >= vals, 1, 0), jnp.where(vm > vals, 1, 0))
        sel_ref[g] = jnp.where(rank < n_sel, 0.0, NEG_INF)

    _reset(m_ref, l_ref, acc_ref)
    blocks_per_tile = TS // SLC_BLOCK

    def vt_of(v_ref, s0):
        return lambda h: v_ref[HEAD_DIM * (h // heads_per_group):HEAD_DIM * (h // heads_per_group + 1),
                               pl.ds(s0, TS)]

    def sel_tile(kt, near):
        s0 = pl.multiple_of(kt * TS, TS)
        off = pl.multiple_of((qi - kt) * TQ, TQ)
        unselected = [
            jnp.concatenate(
                [jnp.broadcast_to(sel_ref[g, pl.ds(kt * blocks_per_tile + i, 1), :], (SLC_BLOCK, TQ))
                 for i in range(blocks_per_tile)], axis=0)
            for g in range(N_KV_B)]
        if near:
            bias_of = lambda h: (toe_ref[h, :, pl.ds(off, TQ)], unselected[h // heads_per_group])
            shift_of = lambda h: 0.0
        else:
            bias_of = lambda h: (unselected[h // heads_per_group],)
            shift_of = lambda h: far_ref[h]
        _attend_tile(ks_ref[pl.ds(s0, TS), :], qb_ref, vt_of(vst_ref, s0), bias_of, shift_of,
                     lg_ref, p_ref, m_ref, l_ref, acc_ref)

    n_far = jnp.maximum(qi - (FAR_TILES - 1), 0)
    lax.fori_loop(0, n_far, lambda kt, c: (sel_tile(kt, False), c)[1], 0)
    lax.fori_loop(n_far, nk, lambda kt, c: (sel_tile(kt, True), c)[1], 0)
    for h in range(N_HEADS_B):
        ow_ref[h] = gt_ref[3 * h:3 * h + 1, :] * oc_ref[h] + gt_ref[3 * h + 1:3 * h + 2, :] * (acc_ref[h] / l_ref[h])

    _reset(m_ref, l_ref, acc_ref)

    def win_tile(back):
        s0 = pl.multiple_of((qi - back) * TS, TS)
        if back == n_win:
            bias_of = lambda h: (toew_ref[h],)
        else:
            bias_of = lambda h: (toe_ref[h, :, back * TQ:(back + 1) * TQ],)
        _attend_tile(kw_ref[pl.ds(s0, TS), :], qb_ref, vt_of(vwt_ref, s0), bias_of, lambda h: 0.0,
                     lg_ref, p_ref, m_ref, l_ref, acc_ref)

    for back in range(n_win, 0, -1):
        pl.when(qi >= back)(functools.partial(win_tile, back))
    win_tile(0)

    _store_transposed(o_ref, lambda h: ow_ref[h] + gt_ref[3 * h + 2:3 * h + 3, :] * (acc_ref[h] / l_ref[h]))


def _nsa_call(far, kc, vct, ks, vst, kw, vwt, qb_t, gt, tab, tabc, slct):
    b, l, _ = ks.shape
    nq = l // TQ
    n_c = kc.shape[1]
    n_sb = slct.shape[0]
    full_rows = pl.BlockSpec((None, l, 128), lambda i, j: (i, 0, 0))
    full_cols = pl.BlockSpec((None, 128, l), lambda i, j: (i, 0, 0))
    qcol = lambda r: pl.BlockSpec((None, r, TQ), lambda i, j: (i, 0, j))
    head_buf = lambda rows: pltpu.VMEM((N_HEADS_B, rows, TQ), F32)
    return pl.pallas_call(
        functools.partial(_nsa_kernel, n_sel=min(N_SLC, n_sb)),
        grid=(b, nq),
        in_specs=[pl.BlockSpec(memory_space=pltpu.SMEM),
                  pl.BlockSpec((None, n_c, 128), lambda i, j: (i, 0, 0)),
                  pl.BlockSpec((None, 128, n_c), lambda i, j: (i, 0, 0)),
                  full_rows, full_cols, full_rows, full_cols,
                  qcol(2 * D_B), qcol(N_HEADS_B * 3),
                  _const_spec(tab.shape), _const_spec(tabc.shape), _const_spec(slct.shape)],
        out_specs=pl.BlockSpec((None, TQ, D_B), lambda i, j: (i, j, 0)),
        out_shape=jax.ShapeDtypeStruct((b, l, D_B), F32),
        scratch_shapes=[pltpu.VMEM((N_HEADS_B, TS, TOE_WIDTH), F32),
                        head_buf(TS),
                        head_buf(CB_ROWS),
                        pltpu.VMEM((n_c, TQ), F32),
                        head_buf(HEAD_DIM),
                        pltpu.VMEM((n_sb, TQ), F32),
                        pltpu.VMEM((N_KV_B, n_sb, TQ), F32),
                        head_buf(HEAD_DIM)] + _attn_scratch(),
        compiler_params=pltpu.CompilerParams(dimension_semantics=("arbitrary", "arbitrary"),
                                             vmem_limit_bytes=VMEM_LIMIT),
        name="nsa",
    )(far, kc, vct, ks, vst, kw, vwt, qb_t, gt, tab, tabc, slct)


def _out_kernel(x_ref, oa_ref, ob_ref, sz_ref, wa_ref, wb_ref, y_ref):
    sz = sz_ref[...]
    ya = (oa_ref[...] * sz[:, :D_A]).astype(BF16)
    yb = (ob_ref[...] * sz[:, D_A:]).astype(BF16)
    y_ref[...] = x_ref[...] + _dot(ya, wa_ref[...]) + _dot(yb, wb_ref[...])


def _out_call(x, oa, ob, sz, wa, wb):
    b, l, d = x.shape
    t = min(T_OUT, l)
    row = lambda w: pl.BlockSpec((None, t, w), lambda i, j: (i, j, 0))
    return pl.pallas_call(
        _out_kernel,
        grid=(b, l // t),
        in_specs=[row(d), row(D_A), row(D_B), row(D_A + D_B), _const_spec(wa.shape), _const_spec(wb.shape)],
        out_specs=row(d),
        out_shape=jax.ShapeDtypeStruct((b, l, d), F32),
        compiler_params=pltpu.CompilerParams(dimension_semantics=("arbitrary", "arbitrary"),
                                             vmem_limit_bytes=VMEM_LIMIT),
        name="out",
    )(x, oa, ob, sz, wa, wb)


def _t5_bucket(dist):
    n = jnp.maximum(dist, 0)
    max_exact = N_BUCKETS // 2
    nf = jnp.maximum(n, 1).astype(jnp.float32)
    log_b = max_exact + (jnp.log(nf / max_exact) / math.log(MAX_DISTANCE / max_exact)
                         * (N_BUCKETS - max_exact)).astype(jnp.int32)
    return jnp.where(n < max_exact, n, jnp.minimum(log_b, N_BUCKETS - 1))


def _bias_vectors(rel_bias):
    assert TOE_PERIOD >= TQ + CB_OFFSET and WINDOW % TS == 0 and WINDOW // TS < FAR_TILES
    k = np.arange(TOE_PERIOD)
    by_dist = rel_bias[_t5_bucket(jnp.arange(TOE_PERIOD))].T.astype(F32)
    far = rel_bias[N_BUCKETS - 1].astype(F32)
    tab = jnp.where(jnp.asarray(k < TOE_WIDTH)[None, :], by_dist, NEG_INF)
    dist_c = np.where(k < TQ, k, k - TOE_PERIOD) + CB_OFFSET
    tabc = jnp.where(jnp.asarray(dist_c >= 0)[None, :],
                     by_dist[:, np.clip(dist_c, 0, TOE_PERIOD - 1)] - far[:, None], 0.0)
    head_major = lambda a: a[N_HEADS_A:, None, :]
    return tab[:N_HEADS_A, None, :], head_major(tab), head_major(tabc), far[:N_HEADS_A], far[N_HEADS_A:]


def _overlap_map_t(n_c, n_sb):
    c_start = np.arange(n_c)[None, :] * CMP_STRIDE
    b_start = np.arange(n_sb)[:, None] * SLC_BLOCK
    ov = np.minimum(c_start + CMP_BLOCK, b_start + SLC_BLOCK) - np.maximum(c_start, b_start)
    return (np.maximum(ov, 0) / CMP_BLOCK).astype(np.float32)


def _pick(w, names, pad_to=None):
    parts = [w[:, PROJ_COLS[n][0]:PROJ_COLS[n][1]] for n in names]
    out = jnp.concatenate(parts, axis=1)
    if pad_to is not None and out.shape[1] < pad_to:
        out = jnp.pad(out, ((0, 0), (0, pad_to - out.shape[1])))
    return out


def _compress_weights(w1, w2, pos):
    half = CMP_BLOCK // 2
    eye = jnp.eye(N_KV_B, dtype=w1.dtype)
    w1r = w1.reshape(2, half, HEAD_DIM, HEAD_DIM)
    w1b = jnp.einsum("sjde,gk->sjgdke", w1r, eye).reshape(2, half * N_KV_B * HEAD_DIM, N_KV_B * HEAD_DIM)
    w2b = jnp.einsum("de,gk->gdke", w2, eye).reshape(N_KV_B * HEAD_DIM, N_KV_B * HEAD_DIM)
    posr = jnp.broadcast_to(pos.reshape(2, half, 1, HEAD_DIM), (2, half, N_KV_B, HEAD_DIM)).reshape(2, -1)
    return w1b.astype(BF16), w2b.astype(BF16), posr.astype(F32)


def kernel(x, rel_bias, norm_g, w_in, b_gate_b, kv_norm_a, w_kv_up_a, q_norm_a, k_norm_a, q_norm_b, k_norm_cmp,
           k_norm_slc, k_norm_win, cmp_pos_k, cmp_pos_v, w_cmp_k1, w_cmp_k2, w_cmp_v1, w_cmp_v2, w_out):
    b, l, d = x.shape
    depth = norm_g.shape[0]
    assert d == D_MODEL and l % T_PROJ == 0 and l % TQ == 0 and l // CMP_STRIDE >= CB_NEAR
    n_c = l // CMP_STRIDE
    n_sb = l // SLC_BLOCK
    tab_a, tab_b, tabc_b, far_a, far_b = _bias_vectors(rel_bias)
    slct = jnp.asarray(_overlap_map_t(n_c, n_sb)).astype(BF16)
    slct = slct * jnp.asarray(np.arange(n_c)[None, :] < n_c - 1, BF16)
    two = lambda v: jnp.tile(v, N_KV_B)[None, :].astype(F32)
    colv = lambda v: v[:, None].astype(F32)
    for layer in range(depth):
        w = w_in[layer]
        wrow = _pick(w, ROW_ORDER, ROW_WIDTH).astype(BF16)
        wcol = _pick(w, COL_ORDER).T.astype(BF16)
        wup = w_kv_up_a[layer]
        n_k = N_KV_A * HEAD_DIM
        (ka, kidx, ks, kw, kcmp, vcmp, sz, qa_t, qi_t, qb_t, va_t, vs_t, vw_t, g_t, w_t) = _proj_call(
            x, norm_g[layer][None, :], wrow, wcol, kv_norm_a[layer][None, :],
            wup[:, :n_k].astype(BF16), wup[:, n_k:].T.astype(BF16),
            two(k_norm_a[layer]), two(k_norm_slc[layer]), two(k_norm_win[layer]),
            colv(q_norm_a[layer]), colv(q_norm_b[layer]), colv(b_gate_b[layer]))
        wk1, wk2, pk = _compress_weights(w_cmp_k1[layer], w_cmp_k2[layer], cmp_pos_k[layer])
        wv1, wv2, pv = _compress_weights(w_cmp_v1[layer], w_cmp_v2[layer], cmp_pos_v[layer])
        row16 = CMP_STRIDE * N_KV_B * HEAD_DIM
        kc, vc_t = _compress_call(kcmp.reshape(b, n_c, row16), vcmp.reshape(b, n_c, row16),
                                  pk, pv, wk1, wv1, wk2, wv2, two(k_norm_cmp[layer]))
        o_a = _dsa_call(far_a, kidx, ka, va_t, qi_t, qa_t, w_t, tab_a)
        o_b = _nsa_call(far_b, kc, vc_t, ks, vs_t, kw, vw_t, qb_t, g_t, tab_b, tabc_b, slct)
        wo = w_out[layer].astype(BF16)
        x = _out_call(x, o_a, o_b, sz, wo[:D_A], wo[D_A:])
    return x
```

```python
import functools
import math

import jax
import jax.numpy as jnp
import numpy as np
from jax import lax
from jax.experimental import pallas as pl
from jax.experimental.pallas import tpu as pltpu

D_MODEL = 1024
HEAD_DIM = 64
N_HEADS_A = 8
N_HEADS_B = 8
D_A = N_HEADS_A * HEAD_DIM
D_B = N_HEADS_B * HEAD_DIM
KV_RANK_A = 128
N_KV_A = 2
N_IDX_HEADS = 8
IDX_DIM = 64
TOPK_A = 256
N_KV_B = 2
D_KV_B = N_KV_B * HEAD_DIM
CMP_BLOCK = 32
CMP_STRIDE = 16
SLC_BLOCK = 64
N_SLC = 16
WINDOW = 512
N_BUCKETS = 32
MAX_DISTANCE = 1024
EPS = 1e-6
NEG_INF = -1e30
FORCE_BONUS = 1e4
ATTN_SCALE = HEAD_DIM ** -0.5
IDX_SCALE = IDX_DIM ** -0.5
IDX_W_SCALE = N_IDX_HEADS ** -0.5

PROJ_SIZES = (
    D_A, KV_RANK_A, N_IDX_HEADS * IDX_DIM, IDX_DIM, N_IDX_HEADS, D_A,
    D_B, D_KV_B, D_KV_B, D_KV_B, D_KV_B, D_KV_B, D_KV_B, N_HEADS_B * 3, D_B,
)
PROJ_NAMES = ("q_a", "c_kv", "q_idx", "k_idx", "w_idx", "z_a",
              "q_b", "k_cmp", "v_cmp", "k_slc", "v_slc", "k_win", "v_win", "g_b", "z_b")
_OFFS = np.concatenate([[0], np.cumsum(PROJ_SIZES)])
PROJ_COLS = {n: (int(_OFFS[i]), int(_OFFS[i + 1])) for i, n in enumerate(PROJ_NAMES)}

TQ = 256
TS = 256
T_PROJ = 512
T_OUT = 512
FAR_DIST = MAX_DISTANCE + 1
FAR_TILES = -(-(FAR_DIST + TS - 1) // TQ)
TOE_WIDTH = FAR_TILES * TQ
TOE_PERIOD = TOE_WIDTH + TS
CB_STEP = TQ // CMP_STRIDE
CB_BACK = -(-(FAR_DIST + CMP_BLOCK - 1 - CMP_STRIDE) // (CMP_STRIDE * CB_STEP)) * CB_STEP
CB_NEAR = CB_BACK + CB_STEP
CB_ROWS = CB_BACK + CB_NEAR
CB_OFFSET = CMP_STRIDE * CB_BACK - (CMP_BLOCK - 1)
LOG2E = math.log2(math.e)
V_ROWS = HEAD_DIM + 16
VT_ROWS = N_KV_A * V_ROWS
LANES = 128
VMEM_LIMIT = 56 * 1024 * 1024
INT_MIN = -(2 ** 31)

F32 = jnp.float32
BF16 = jnp.bfloat16
NT_DIMS = (((1,), (1,)), ((), ()))

ROW_ORDER = ("c_kv", "z_a", "z_b", "k_cmp", "v_cmp", "k_slc", "k_win", "k_idx")
ROW_WIDTH = 1792
COL_ORDER = ("q_a", "q_idx", "q_b", "v_slc", "v_win", "g_b", "w_idx")


def _layout(order):
    offs, o = {}, 0
    for n in order:
        w = PROJ_COLS[n][1] - PROJ_COLS[n][0]
        offs[n] = (o, o + w)
        o += w
    return offs, o


ROW_OFFS, _ROW_USED = _layout(ROW_ORDER)
COL_OFFS, COL_ROWS = _layout(COL_ORDER)


def _dot(a, b):
    return jnp.dot(a, b, preferred_element_type=F32)


def _dot_nt(a, b):
    return lax.dot_general(a, b, NT_DIMS, preferred_element_type=F32)


def _rms_rows(a, gain):
    ms = jnp.mean(a * a, axis=-1, keepdims=True)
    return a * lax.rsqrt(ms + EPS) * gain


def _group_rms_rows(a, gain):
    lane = lax.broadcasted_iota(jnp.int32, a.shape, 1)
    sq = a * a
    lo = lane < HEAD_DIM
    s_lo = jnp.sum(jnp.where(lo, sq, 0.0), axis=-1, keepdims=True)
    s_hi = jnp.sum(jnp.where(lo, 0.0, sq), axis=-1, keepdims=True)
    ms = jnp.where(lo, s_lo, s_hi) * (1.0 / HEAD_DIM)
    return a * lax.rsqrt(ms + EPS) * gain


def _proj_kernel(x_ref, g_ref, wrow_ref, wcol_ref, kvn_ref, wup_ref, wupv_ref, kna_ref, kns_ref, knw_ref,
                 qna_ref, qnb_ref, bg_ref,
                 ka_ref, kidx_ref, ks_ref, kw_ref, kcmp_ref, vcmp_ref, sz_ref,
                 qa_ref, qi_ref, qb_ref, va_ref, vs_ref, vw_ref, gt_ref, wt_ref):
    t = x_ref.shape[0]
    h = _rms_rows(x_ref[...], g_ref[...]).astype(BF16)
    pr = _dot(h, wrow_ref[...])

    def rows(name):
        a, b = ROW_OFFS[name]
        return pr[:, a:b]

    one_row = jnp.where(lax.broadcasted_iota(jnp.int32, (V_ROWS - HEAD_DIM, t), 0) == 0, 1.0, 0.0).astype(BF16)

    def with_ones(vt):
        vt = vt.astype(BF16)
        return jnp.concatenate([vt[:HEAD_DIM], one_row, vt[HEAD_DIM:], one_row], axis=0)

    ckvn = _rms_rows(rows("c_kv"), kvn_ref[...]).astype(BF16)
    k_up = _dot(ckvn, wup_ref[...])
    ka_ref[...] = _group_rms_rows(k_up, kna_ref[...]).astype(BF16)
    va_ref[...] = with_ones(_dot_nt(wupv_ref[...], ckvn))
    za, zb = ROW_OFFS["z_a"][0], ROW_OFFS["z_b"][1]
    z = pr[:, za:zb]
    sz_ref[...] = z * jax.nn.sigmoid(z)
    kcmp_ref[...] = rows("k_cmp")
    vcmp_ref[...] = rows("v_cmp")
    ks_ref[...] = _group_rms_rows(rows("k_slc"), kns_ref[...]).astype(BF16)
    kw_ref[...] = _group_rms_rows(rows("k_win"), knw_ref[...]).astype(BF16)
    kidx_ref[...] = rows("k_idx").astype(BF16)

    pt = _dot_nt(wcol_ref[...], h)

    def cols(name):
        a, b = COL_OFFS[name]
        return pt[a:b, :]

    def qnorm_padded(q, gain_col):
        q = q.reshape(N_HEADS_A, HEAD_DIM, t)
        ms = jnp.mean(q * q, axis=1, keepdims=True)
        qn = (q * lax.rsqrt(ms + EPS) * gain_col[None] * (ATTN_SCALE * LOG2E)).astype(BF16)
        zeros = jnp.zeros((HEAD_DIM, t), BF16)
        parts = []
        for hh in range(N_HEADS_A):
            parts += [qn[hh], zeros] if hh < N_HEADS_A // N_KV_A else [zeros, qn[hh]]
        return jnp.concatenate(parts, axis=0)

    qa_ref[...] = qnorm_padded(cols("q_a"), qna_ref[...])
    qb_ref[...] = qnorm_padded(cols("q_b"), qnb_ref[...])
    qi_ref[...] = (cols("q_idx") * IDX_SCALE).astype(BF16)
    vs_ref[...] = with_ones(cols("v_slc"))
    vw_ref[...] = with_ones(cols("v_win"))
    gt_ref[...] = jax.nn.sigmoid(cols("g_b") + bg_ref[...])
    wt_ref[...] = cols("w_idx") * IDX_W_SCALE


def _const_spec(shape):
    nd = len(shape)
    return pl.BlockSpec(shape, lambda *_: (0,) * nd)


def _proj_call(x, g, wrow, wcol, kvn, wup, wupv, kna, kns, knw, qna, qnb, bg):
    b, l, _ = x.shape
    t = min(T_PROJ, l)
    grid = (b, l // t)
    row = lambda w: pl.BlockSpec((None, t, w), lambda i, j: (i, j, 0))
    col = lambda r: pl.BlockSpec((None, r, t), lambda i, j: (i, 0, j))
    rs = lambda w, dt: jax.ShapeDtypeStruct((b, l, w), dt)
    cs = lambda r, dt: jax.ShapeDtypeStruct((b, r, l), dt)
    consts = (g, wrow, wcol, kvn, wup, wupv, kna, kns, knw, qna, qnb, bg)
    return pl.pallas_call(
        _proj_kernel,
        grid=grid,
        in_specs=[row(D_MODEL)] + [_const_spec(c.shape) for c in consts],
        out_specs=[row(128), row(IDX_DIM), row(128), row(128), row(128), row(128), row(D_A + D_B),
                   col(2 * D_A), col(N_IDX_HEADS * IDX_DIM), col(2 * D_B), col(VT_ROWS), col(VT_ROWS), col(VT_ROWS),
                   col(N_HEADS_B * 3), col(N_IDX_HEADS)],
        out_shape=[rs(128, BF16), rs(IDX_DIM, BF16), rs(128, BF16), rs(128, BF16), rs(128, F32), rs(128, F32),
                   rs(D_A + D_B, F32),
                   cs(2 * D_A, BF16), cs(N_IDX_HEADS * IDX_DIM, BF16), cs(2 * D_B, BF16), cs(VT_ROWS, BF16),
                   cs(VT_ROWS, BF16), cs(VT_ROWS, BF16), cs(N_HEADS_B * 3, F32), cs(N_IDX_HEADS, F32)],
        compiler_params=pltpu.CompilerParams(dimension_semantics=("arbitrary", "arbitrary"),
                                             vmem_limit_bytes=VMEM_LIMIT),
        name="proj",
    )(x, *consts)


def _compress_kernel(kr_ref, vr_ref, pk_ref, pv_ref, wk1_ref, wv1_ref, wk2_ref, wv2_ref, knc_ref,
                     kc_ref, vct_ref):
    n = kr_ref.shape[0]
    row = lax.broadcasted_iota(jnp.int32, (n, 128), 0)
    valid = row < n - 1

    def branch(r_ref, p_ref, w1_ref, w2_ref):
        r = r_ref[...]
        a = _dot((r + p_ref[0:1, :]).astype(BF16), w1_ref[0])
        b = _dot((r + p_ref[1:2, :]).astype(BF16), w1_ref[1])
        pre = a + pltpu.roll(b, n - 1, axis=0)
        return _dot(jax.nn.gelu(pre).astype(BF16), w2_ref[...])

    kc = _group_rms_rows(branch(kr_ref, pk_ref, wk1_ref, wk2_ref), knc_ref[...])
    kc_ref[...] = jnp.where(valid, kc, 0.0).astype(BF16)
    vc = jnp.where(valid, branch(vr_ref, pv_ref, wv1_ref, wv2_ref), 0.0)
    vct_ref[...] = vc.T.astype(BF16)


def _compress_call(kr, vr, pk, pv, wk1, wv1, wk2, wv2, knc):
    b, n, w = kr.shape
    blk = pl.BlockSpec((None, n, w), lambda i: (i, 0, 0))
    consts = (pk, pv, wk1, wv1, wk2, wv2, knc)
    return pl.pallas_call(
        _compress_kernel,
        grid=(b,),
        in_specs=[blk, blk] + [_const_spec(c.shape) for c in consts],
        out_specs=[pl.BlockSpec((None, n, 128), lambda i: (i, 0, 0)),
                   pl.BlockSpec((None, 128, n), lambda i: (i, 0, 0))],
        out_shape=[jax.ShapeDtypeStruct((b, n, 128), BF16), jax.ShapeDtypeStruct((b, 128, n), BF16)],
        compiler_params=pltpu.CompilerParams(dimension_semantics=("arbitrary",), vmem_limit_bytes=VMEM_LIMIT),
        name="compress",
    )(kr, vr, *consts)


def _attend_tile(k_t, q_ref, vt_of, bias_of, shift_of, lg_ref, p_ref, m_ref, a_ref, acc_ref):
    heads = range(N_HEADS_A)
    for h in heads:
        lg_ref[h] = _dot(k_t, q_ref[128 * h:128 * (h + 1), :])
    for h in heads:
        shift = shift_of(h)
        for c in range(TQ // LANES):
            cs = slice(LANES * c, LANES * (c + 1))
            lg = lg_ref[h, :, cs]
            for term in bias_of(h, c):
                lg = lg + term
            m_old = m_ref[h, :, cs]
            m_new = jnp.maximum(m_old, jnp.max(lg, axis=0, keepdims=True) + shift)
            a_ref[h, :, cs] = jnp.exp2(m_old - m_new)
            m_ref[h, :, cs] = m_new
            p_ref[h, :, cs] = jnp.exp2(lg - (m_new - shift)).astype(BF16)
    for h in heads:
        acc_ref[h] = a_ref[h] * acc_ref[h] + _dot(vt_of(h), p_ref[h])


def _build_toeplitz(tab_ref, toe_ref):
    period = tab_ref.shape[-1]
    for h in range(toe_ref.shape[0]):
        rows = jnp.broadcast_to(tab_ref[h], (TS, period))
        toe_ref[h] = pltpu.roll(rows, 0, 1, stride=1, stride_axis=0)[:, :toe_ref.shape[-1]]


def _first_step():
    return jnp.logical_and(pl.program_id(0) == 0, pl.program_id(1) == 0)


def _reset(m_ref, acc_ref):
    m_ref[...] = jnp.full(m_ref.shape, NEG_INF, F32)
    acc_ref[...] = jnp.zeros(acc_ref.shape, F32)


def _normalized(acc_ref, h):
    return acc_ref[h, :HEAD_DIM, :] / acc_ref[h, HEAD_DIM:HEAD_DIM + 1, :]


def _store_transposed(o_ref, head_out):
    for c in range(N_HEADS_A // 2):
        pair = jnp.concatenate([head_out(2 * c), head_out(2 * c + 1)], axis=0)
        o_ref[:, 128 * c:128 * (c + 1)] = pair.T


def _dsa_kernel(far_ref, kidx_ref, ka_ref, vat_ref, qi_ref, qa_ref, wt_ref, tab_ref, o_ref,
                keys_ref, toe_ref, lg_ref, p_ref, m_ref, a_ref, acc_ref, *, k_top):
    @pl.when(_first_step())
    def _():
        _build_toeplitz(tab_ref, toe_ref)

    qi = pl.program_id(1)
    nk = qi + 1
    q0 = qi * TQ
    t_pos = q0 + lax.broadcasted_iota(jnp.int32, (TS, TQ), 1)
    s_off = lax.broadcasted_iota(jnp.int32, (TS, TQ), 0)

    def score_body(kt, carry):
        s0 = pl.multiple_of(kt * TS, TS)
        kx = kidx_ref[pl.ds(s0, TS), :]
        acc = jnp.zeros((TS, TQ), F32)
        for h in range(N_IDX_HEADS):
            d = _dot(kx, qi_ref[IDX_DIM * h:IDX_DIM * (h + 1), :])
            acc = acc + wt_ref[h:h + 1, :] * jnp.maximum(d, 0.0)
        acc = jnp.where(acc == 0.0, 0.0, acc)
        bits = pltpu.bitcast(acc, jnp.int32)
        key = bits ^ ((bits >> 31) & jnp.int32(0x7FFFFFFF))
        keys_ref[pl.ds(s0, TS), :] = jnp.where(s0 + s_off <= t_pos, key, INT_MIN)
        return carry

    lax.fori_loop(0, nk, score_body, 0)

    def count(pred):
        def body(kt, c):
            s0 = pl.multiple_of(kt * TS, TS)
            hit = jnp.where(pred(keys_ref[pl.ds(s0, TS), :]), 1, 0)
            return c + jnp.sum(hit.reshape(TS // 8, 8, TQ), axis=0)
        c8 = lax.fori_loop(0, nk, body, jnp.zeros((8, TQ), jnp.int32))
        return jnp.sum(c8, axis=0, keepdims=True)

    def bisect(it, thr_u):
        cand_u = thr_u | lax.shift_left(jnp.int32(1), 31 - it)
        cand_s = cand_u ^ INT_MIN
        n_ge = count(lambda tile: tile >= cand_s)
        return jnp.where(n_ge >= k_top, cand_u, thr_u)

    thr_u = lax.fori_loop(0, 32, bisect, jnp.zeros((1, TQ), jnp.int32))
    thr = jnp.maximum(thr_u ^ INT_MIN, INT_MIN + 1)

    n_gt = count(lambda tile: tile > thr)
    n_ge = count(lambda tile: tile >= thr)
    tie = n_ge > k_top
    need = jnp.where(tie, (k_top - n_gt).astype(F32), float(2 ** 30))

    @pl.when(jnp.max(jnp.where(tie, 1, 0)) > 0)
    def _():
        lower = (lax.broadcasted_iota(jnp.int32, (TS, TS), 1)
                 <= lax.broadcasted_iota(jnp.int32, (TS, TS), 0))
        tri = jnp.where(lower, 1.0, 0.0).astype(BF16)

        def body(kt, run):
            s0 = pl.multiple_of(kt * TS, TS)
            tile = keys_ref[pl.ds(s0, TS), :]
            eq = tile == thr
            pref = _dot(tri, jnp.where(eq, 1.0, 0.0).astype(BF16))
            surplus = jnp.where(eq, run + pref, 0.0) > need
            keys_ref[pl.ds(s0, TS), :] = jnp.where(surplus, thr - 1, tile)
            return run + pref[TS - 1:TS, :]

        lax.fori_loop(0, nk, body, jnp.zeros((1, TQ), F32))

    _reset(m_ref, acc_ref)

    def att_tile(kt, near):
        s0 = pl.multiple_of(kt * TS, TS)
        off = pl.multiple_of((qi - kt) * TQ, TQ)
        unselected = jnp.where(keys_ref[pl.ds(s0, TS), :] >= thr, 0.0, NEG_INF)
        unsel_of = lambda c: unselected[:, LANES * c:LANES * (c + 1)]
        if near:
            bias_of = lambda h, c: (toe_ref[h, :, pl.ds(pl.multiple_of(off + LANES * c, LANES), LANES)],
                                    unsel_of(c))
            shift_of = lambda h: 0.0
        else:
            bias_of = lambda h, c: (unsel_of(c),)
            shift_of = lambda h: far_ref[h]
        _attend_tile(ka_ref[pl.ds(s0, TS), :], qa_ref, _vt_of(vat_ref, s0), bias_of, shift_of,
                     lg_ref, p_ref, m_ref, a_ref, acc_ref)

    n_far = jnp.maximum(qi - (FAR_TILES - 1), 0)
    lax.fori_loop(0, n_far, lambda kt, c: (att_tile(kt, False), c)[1], 0)
    lax.fori_loop(n_far, nk, lambda kt, c: (att_tile(kt, True), c)[1], 0)

    _store_transposed(o_ref, functools.partial(_normalized, acc_ref))


def _vt_of(v_ref, s0):
    heads_per_group = N_HEADS_A // N_KV_A
    return lambda h: v_ref[V_ROWS * (h // heads_per_group):V_ROWS * (h // heads_per_group + 1), pl.ds(s0, TS)]


def _attn_scratch():
    return [pltpu.VMEM((N_HEADS_A, TS, TQ), F32),
            pltpu.VMEM((N_HEADS_A, TS, TQ), BF16),
            pltpu.VMEM((N_HEADS_A, 1, TQ), F32),
            pltpu.VMEM((N_HEADS_A, 1, TQ), F32),
            pltpu.VMEM((N_HEADS_A, V_ROWS, TQ), F32)]


def _dsa_call(far, kidx, ka, vat, qi_t, qa_t, wt, tab):
    b, l, _ = ka.shape
    nq = l // TQ
    full_rows = lambda w: pl.BlockSpec((None, l, w), lambda i, j: (i, 0, 0))
    qcol = lambda r: pl.BlockSpec((None, r, TQ), lambda i, j: (i, 0, j))
    return pl.pallas_call(
        functools.partial(_dsa_kernel, k_top=min(TOPK_A, l // 4)),
        grid=(b, nq),
        in_specs=[pl.BlockSpec(memory_space=pltpu.SMEM),
                  full_rows(IDX_DIM), full_rows(128),
                  pl.BlockSpec((None, VT_ROWS, l), lambda i, j: (i, 0, 0)),
                  qcol(N_IDX_HEADS * IDX_DIM), qcol(2 * D_A), qcol(N_IDX_HEADS),
                  _const_spec(tab.shape)],
        out_specs=pl.BlockSpec((None, TQ, D_A), lambda i, j: (i, j, 0)),
        out_shape=jax.ShapeDtypeStruct((b, l, D_A), F32),
        scratch_shapes=[pltpu.VMEM((l, TQ), jnp.int32),
                        pltpu.VMEM((N_HEADS_A, TS, TOE_WIDTH), F32)] + _attn_scratch(),
        compiler_params=pltpu.CompilerParams(dimension_semantics=("arbitrary", "arbitrary"),
                                             vmem_limit_bytes=VMEM_LIMIT),
        name="dsa",
    )(far, kidx, ka, vat, qi_t, qa_t, wt, tab)


def _nsa_kernel(far_ref, kc_ref, vct_ref, ks_ref, vst_ref, kw_ref, vwt_ref, qb_ref, gt_ref,
                tab_ref, tabc_ref, slct_ref, o_ref,
                toe_ref, toew_ref, cbn_ref, lgc_ref, oc_ref, vals_ref, sel_ref, ow_ref,
                lg_ref, p_ref, m_ref, a_ref, acc_ref, *, n_sel):
    n_win = WINDOW // TS

    @pl.when(_first_step())
    def _():
        _build_toeplitz(tab_ref, toe_ref)
        inside = (lax.broadcasted_iota(jnp.int32, (TS, TQ), 1) < lax.broadcasted_iota(jnp.int32, (TS, TQ), 0))
        for h in range(N_HEADS_B):
            toew_ref[h] = jnp.where(inside, toe_ref[h, :, n_win * TQ:(n_win + 1) * TQ], NEG_INF)
            rows = jnp.broadcast_to(tabc_ref[h], (CB_ROWS, TOE_PERIOD))
            cbn_ref[h] = pltpu.roll(rows, 0, 1, stride=CMP_STRIDE, stride_axis=0)[:, :TQ]

    qi = pl.program_id(1)
    nk = qi + 1
    q0 = qi * TQ
    n_c = kc_ref.shape[0]
    n_sb = slct_ref.shape[0]
    heads_per_group = N_HEADS_B // N_KV_B
    cb_row0 = pl.multiple_of(jnp.maximum(qi * CB_STEP - CB_BACK, 0), CB_STEP)
    cb_tab0 = pl.multiple_of(CB_BACK - (qi * CB_STEP - cb_row0), CB_STEP)

    t_c = q0 + lax.broadcasted_iota(jnp.int32, (n_c, TQ), 1)
    end_c = lax.broadcasted_iota(jnp.int32, (n_c, TQ), 0) * CMP_STRIDE + (CMP_BLOCK - 1)
    mask_c = t_c >= end_c
    kc = kc_ref[...]
    slct = slct_ref[...]
    t_b = q0 + lax.broadcasted_iota(jnp.int32, (n_sb, TQ), 1)
    blk = lax.broadcasted_iota(jnp.int32, (n_sb, TQ), 0)
    jt = t_b >> int(math.log2(SLC_BLOCK))
    forced = (blk == 0) | (blk == jt) | (blk == jt - 1)
    for g in range(N_KV_B):
        imp = jnp.zeros((n_sb, TQ), F32)
        for r in range(heads_per_group):
            h = g * heads_per_group + r
            lgc_ref[...] = _dot(kc, qb_ref[128 * h:128 * (h + 1), :]) + far_ref[h]
            lgc_ref[pl.ds(cb_row0, CB_NEAR), :] = (lgc_ref[pl.ds(cb_row0, CB_NEAR), :]
                                                   + cbn_ref[h, pl.ds(cb_tab0, CB_NEAR), :])
            lg = jnp.where(mask_c, lgc_ref[...], NEG_INF)
            e = jnp.where(mask_c, jnp.exp2(lg - jnp.max(lg, axis=0, keepdims=True)), 0.0)
            den = jnp.sum(e, axis=0, keepdims=True)
            p = e * jnp.where(den > 0.0, 1.0 / den, 0.0)
            p_hi = p.astype(BF16)
            p_lo = (p - p_hi.astype(F32)).astype(BF16)
            oc_ref[h] = _dot(vct_ref[HEAD_DIM * g:HEAD_DIM * (g + 1), :], p_hi)
            imp = imp + _dot(slct, p_hi) + _dot(slct, p_lo)
        vals = imp + jnp.where(forced, FORCE_BONUS, 0.0)
        vals = jnp.where(blk * SLC_BLOCK <= t_b, vals, NEG_INF)
        vals_ref[...] = vals
        rank = jnp.zeros((n_sb, TQ), jnp.int32)
        for m in range(n_sb):
            vm = vals_ref[m:m + 1, :]
            rank = rank + jnp.where(blk > m, jnp.where(vm >= vals, 1, 0), jnp.where(vm > vals, 1, 0))
        sel_ref[g] = jnp.where(rank < n_sel, 0.0, NEG_INF)

    _reset(m_ref, acc_ref)
    blocks_per_tile = TS // SLC_BLOCK

    def sel_tile(kt, near):
        s0 = pl.multiple_of(kt * TS, TS)
        off = pl.multiple_of((qi - kt) * TQ, TQ)

        def unsel_of(h, c):
            g = h // heads_per_group
            return jnp.concatenate(
                [jnp.broadcast_to(sel_ref[g, pl.ds(kt * blocks_per_tile + i, 1), :][:, LANES * c:LANES * (c + 1)],
                                  (SLC_BLOCK, LANES)) for i in range(blocks_per_tile)], axis=0)

        if near:
            bias_of = lambda h, c: (toe_ref[h, :, pl.ds(pl.multiple_of(off + LANES * c, LANES), LANES)],
                                    unsel_of(h, c))
            shift_of = lambda h: 0.0
        else:
            bias_of = lambda h, c: (unsel_of(h, c),)
            shift_of = lambda h: far_ref[h]
        _attend_tile(ks_ref[pl.ds(s0, TS), :], qb_ref, _vt_of(vst_ref, s0), bias_of, shift_of,
                     lg_ref, p_ref, m_ref, a_ref, acc_ref)

    n_far = jnp.maximum(qi - (FAR_TILES - 1), 0)
    lax.fori_loop(0, n_far, lambda kt, c: (sel_tile(kt, False), c)[1], 0)
    lax.fori_loop(n_far, nk, lambda kt, c: (sel_tile(kt, True), c)[1], 0)
    for h in range(N_HEADS_B):
        ow_ref[h] = (gt_ref[3 * h:3 * h + 1, :] * oc_ref[h]
                     + gt_ref[3 * h + 1:3 * h + 2, :] * _normalized(acc_ref, h))

    _reset(m_ref, acc_ref)

    def win_tile(back):
        s0 = pl.multiple_of((qi - back) * TS, TS)
        if back == n_win:
            bias_of = lambda h, c: (toew_ref[h, :, LANES * c:LANES * (c + 1)],)
        else:
            bias_of = lambda h, c: (toe_ref[h, :, back * TQ + LANES * c:back * TQ + LANES * (c + 1)],)
        _attend_tile(kw_ref[pl.ds(s0, TS), :], qb_ref, _vt_of(vwt_ref, s0), bias_of, lambda h: 0.0,
                     lg_ref, p_ref, m_ref, a_ref, acc_ref)

    for back in range(n_win, 0, -1):
        pl.when(qi >= back)(functools.partial(win_tile, back))
    win_tile(0)

    _store_transposed(o_ref, lambda h: ow_ref[h] + gt_ref[3 * h + 2:3 * h + 3, :] * _normalized(acc_ref, h))


def _nsa_call(far, kc, vct, ks, vst, kw, vwt, qb_t, gt, tab, tabc, slct):
    b, l, _ = ks.shape
    nq = l // TQ
    n_c = kc.shape[1]
    n_sb = slct.shape[0]
    full_rows = pl.BlockSpec((None, l, 128), lambda i, j: (i, 0, 0))
    full_cols = pl.BlockSpec((None, VT_ROWS, l), lambda i, j: (i, 0, 0))
    qcol = lambda r: pl.BlockSpec((None, r, TQ), lambda i, j: (i, 0, j))
    head_buf = lambda rows: pltpu.VMEM((N_HEADS_B, rows, TQ), F32)
    return pl.pallas_call(
        functools.partial(_nsa_kernel, n_sel=min(N_SLC, n_sb)),
        grid=(b, nq),
        in_specs=[pl.BlockSpec(memory_space=pltpu.SMEM),
                  pl.BlockSpec((None, n_c, 128), lambda i, j: (i, 0, 0)),
                  pl.BlockSpec((None, 128, n_c), lambda i, j: (i, 0, 0)),
                  full_rows, full_cols, full_rows, full_cols,
                  qcol(2 * D_B), qcol(N_HEADS_B * 3),
                  _const_spec(tab.shape), _const_spec(tabc.shape), _const_spec(slct.shape)],
        out_specs=pl.BlockSpec((None, TQ, D_B), lambda i, j: (i, j, 0)),
        out_shape=jax.ShapeDtypeStruct((b, l, D_B), F32),
        scratch_shapes=[pltpu.VMEM((N_HEADS_B, TS, TOE_WIDTH), F32),
                        head_buf(TS),
                        head_buf(CB_ROWS),
                        pltpu.VMEM((n_c, TQ), F32),
                        head_buf(HEAD_DIM),
                        pltpu.VMEM((n_sb, TQ), F32),
                        pltpu.VMEM((N_KV_B, n_sb, TQ), F32),
                        head_buf(HEAD_DIM)] + _attn_scratch(),
        compiler_params=pltpu.CompilerParams(dimension_semantics=("arbitrary", "arbitrary"),
                                             vmem_limit_bytes=VMEM_LIMIT),
        name="nsa",
    )(far, kc, vct, ks, vst, kw, vwt, qb_t, gt, tab, tabc, slct)


def _out_kernel(x_ref, oa_ref, ob_ref, sz_ref, wa_ref, wb_ref, y_ref):
    sz = sz_ref[...]
    ya = (oa_ref[...] * sz[:, :D_A]).astype(BF16)
    yb = (ob_ref[...] * sz[:, D_A:]).astype(BF16)
    y_ref[...] = x_ref[...] + _dot(ya, wa_ref[...]) + _dot(yb, wb_ref[...])


def _out_call(x, oa, ob, sz, wa, wb):
    b, l, d = x.shape
    t = min(T_OUT, l)
    row = lambda w: pl.BlockSpec((None, t, w), lambda i, j: (i, j, 0))
    return pl.pallas_call(
        _out_kernel,
        grid=(b, l // t),
        in_specs=[row(d), row(D_A), row(D_B), row(D_A + D_B), _const_spec(wa.shape), _const_spec(wb.shape)],
        out_specs=row(d),
        out_shape=jax.ShapeDtypeStruct((b, l, d), F32),
        compiler_params=pltpu.CompilerParams(dimension_semantics=("arbitrary", "arbitrary"),
                                             vmem_limit_bytes=VMEM_LIMIT),
        name="out",
    )(x, oa, ob, sz, wa, wb)


def _t5_bucket(dist):
    n = jnp.maximum(dist, 0)
    max_exact = N_BUCKETS // 2
    nf = jnp.maximum(n, 1).astype(jnp.float32)
    log_b = max_exact + (jnp.log(nf / max_exact) / math.log(MAX_DISTANCE / max_exact)
                         * (N_BUCKETS - max_exact)).astype(jnp.int32)
    return jnp.where(n < max_exact, n, jnp.minimum(log_b, N_BUCKETS - 1))


def _bias_vectors(rel_bias):
    assert TOE_PERIOD >= TQ + CB_OFFSET and WINDOW % TS == 0 and WINDOW // TS < FAR_TILES
    k = np.arange(TOE_PERIOD)
    by_dist = rel_bias[_t5_bucket(jnp.arange(TOE_PERIOD))].T.astype(F32) * LOG2E
    far = rel_bias[N_BUCKETS - 1].astype(F32) * LOG2E
    tab = jnp.where(jnp.asarray(k < TOE_WIDTH)[None, :], by_dist, NEG_INF)
    dist_c = np.where(k < TQ, k, k - TOE_PERIOD) + CB_OFFSET
    tabc = jnp.where(jnp.asarray(dist_c >= 0)[None, :],
                     by_dist[:, np.clip(dist_c, 0, TOE_PERIOD - 1)] - far[:, None], 0.0)
    head_major = lambda a: a[N_HEADS_A:, None, :]
    return tab[:N_HEADS_A, None, :], head_major(tab), head_major(tabc), far[:N_HEADS_A], far[N_HEADS_A:]


def _overlap_map_t(n_c, n_sb):
    c_start = np.arange(n_c)[None, :] * CMP_STRIDE
    b_start = np.arange(n_sb)[:, None] * SLC_BLOCK
    ov = np.minimum(c_start + CMP_BLOCK, b_start + SLC_BLOCK) - np.maximum(c_start, b_start)
    return (np.maximum(ov, 0) / CMP_BLOCK).astype(np.float32)


def _pick(w, names, pad_to=None):
    parts = [w[:, PROJ_COLS[n][0]:PROJ_COLS[n][1]] for n in names]
    out = jnp.concatenate(parts, axis=1)
    if pad_to is not None and out.shape[1] < pad_to:
        out = jnp.pad(out, ((0, 0), (0, pad_to - out.shape[1])))
    return out


def _compress_weights(w1, w2, pos):
    half = CMP_BLOCK // 2
    eye = jnp.eye(N_KV_B, dtype=w1.dtype)
    w1r = w1.reshape(2, half, HEAD_DIM, HEAD_DIM)
    w1b = jnp.einsum("sjde,gk->sjgdke", w1r, eye).reshape(2, half * N_KV_B * HEAD_DIM, N_KV_B * HEAD_DIM)
    w2b = jnp.einsum("de,gk->gdke", w2, eye).reshape(N_KV_B * HEAD_DIM, N_KV_B * HEAD_DIM)
    posr = jnp.broadcast_to(pos.reshape(2, half, 1, HEAD_DIM), (2, half, N_KV_B, HEAD_DIM)).reshape(2, -1)
    return w1b.astype(BF16), w2b.astype(BF16), posr.astype(F32)


def kernel(x, rel_bias, norm_g, w_in, b_gate_b, kv_norm_a, w_kv_up_a, q_norm_a, k_norm_a, q_norm_b, k_norm_cmp,
           k_norm_slc, k_norm_win, cmp_pos_k, cmp_pos_v, w_cmp_k1, w_cmp_k2, w_cmp_v1, w_cmp_v2, w_out):
    b, l, d = x.shape
    depth = norm_g.shape[0]
    assert d == D_MODEL and l % T_PROJ == 0 and l % TQ == 0 and l // CMP_STRIDE >= CB_NEAR
    n_c = l // CMP_STRIDE
    n_sb = l // SLC_BLOCK
    tab_a, tab_b, tabc_b, far_a, far_b = _bias_vectors(rel_bias)
    slct = jnp.asarray(_overlap_map_t(n_c, n_sb)).astype(BF16)
    slct = slct * jnp.asarray(np.arange(n_c)[None, :] < n_c - 1, BF16)
    two = lambda v: jnp.tile(v, N_KV_B)[None, :].astype(F32)
    colv = lambda v: v[:, None].astype(F32)
    for layer in range(depth):
        w = w_in[layer]
        wrow = _pick(w, ROW_ORDER, ROW_WIDTH).astype(BF16)
        wcol = _pick(w, COL_ORDER).T.astype(BF16)
        wup = w_kv_up_a[layer]
        n_k = N_KV_A * HEAD_DIM
        (ka, kidx, ks, kw, kcmp, vcmp, sz, qa_t, qi_t, qb_t, va_t, vs_t, vw_t, g_t, w_t) = _proj_call(
            x, norm_g[layer][None, :], wrow, wcol, kv_norm_a[layer][None, :],
            wup[:, :n_k].astype(BF16), wup[:, n_k:].T.astype(BF16),
            two(k_norm_a[layer]), two(k_norm_slc[layer]), two(k_norm_win[layer]),
            colv(q_norm_a[layer]), colv(q_norm_b[layer]), colv(b_gate_b[layer]))
        wk1, wk2, pk = _compress_weights(w_cmp_k1[layer], w_cmp_k2[layer], cmp_pos_k[layer])
        wv1, wv2, pv = _compress_weights(w_cmp_v1[layer], w_cmp_v2[layer], cmp_pos_v[layer])
        row16 = CMP_STRIDE * N_KV_B * HEAD_DIM
        kc, vc_t = _compress_call(kcmp.reshape(b, n_c, row16), vcmp.reshape(b, n_c, row16),
                                  pk, pv, wk1, wv1, wk2, wv2, two(k_norm_cmp[layer]))
        o_a = _dsa_call(far_a, kidx, ka, va_t, qi_t, qa_t, w_t, tab_a)
        o_b = _nsa_call(far_b, kc, vc_t, ks, vs_t, kw, vw_t, qb_t, g_t, tab_b, tabc_b, slct)
        wo = w_out[layer].astype(BF16)
        x = _out_call(x, o_a, o_b, sz, wo[:D_A], wo[D_A:])
    return x
```

```python
import functools
import math

import jax
import jax.numpy as jnp
import numpy as np
from jax import lax
from jax.experimental import pallas as pl
from jax.experimental.pallas import tpu as pltpu

D_MODEL = 1024
HEAD_DIM = 64
N_HEADS_A = 8
N_HEADS_B = 8
D_A = N_HEADS_A * HEAD_DIM
D_B = N_HEADS_B * HEAD_DIM
KV_RANK_A = 128
N_KV_A = 2
N_IDX_HEADS = 8
IDX_DIM = 64
TOPK_A = 256
N_KV_B = 2
D_KV_B = N_KV_B * HEAD_DIM
CMP_BLOCK = 32
CMP_STRIDE = 16
SLC_BLOCK = 64
N_SLC = 16
WINDOW = 512
N_BUCKETS = 32
MAX_DISTANCE = 1024
EPS = 1e-6
NEG_INF = -1e30
FORCE_BONUS = 1e4
ATTN_SCALE = HEAD_DIM ** -0.5
IDX_SCALE = IDX_DIM ** -0.5
IDX_W_SCALE = N_IDX_HEADS ** -0.5

PROJ_SIZES = (
    D_A, KV_RANK_A, N_IDX_HEADS * IDX_DIM, IDX_DIM, N_IDX_HEADS, D_A,
    D_B, D_KV_B, D_KV_B, D_KV_B, D_KV_B, D_KV_B, D_KV_B, N_HEADS_B * 3, D_B,
)
PROJ_NAMES = ("q_a", "c_kv", "q_idx", "k_idx", "w_idx", "z_a",
              "q_b", "k_cmp", "v_cmp", "k_slc", "v_slc", "k_win", "v_win", "g_b", "z_b")
_OFFS = np.concatenate([[0], np.cumsum(PROJ_SIZES)])
PROJ_COLS = {n: (int(_OFFS[i]), int(_OFFS[i + 1])) for i, n in enumerate(PROJ_NAMES)}

TQ = 256
TS = 256
T_PROJ = 512
T_OUT = 512
FAR_DIST = MAX_DISTANCE + 1
FAR_TILES = -(-(FAR_DIST + TS - 1) // TQ)
TOE_WIDTH = FAR_TILES * TQ
TOE_PERIOD = TOE_WIDTH + TS
CB_STEP = TQ // CMP_STRIDE
CB_BACK = -(-(FAR_DIST + CMP_BLOCK - 1 - CMP_STRIDE) // (CMP_STRIDE * CB_STEP)) * CB_STEP
CB_NEAR = CB_BACK + CB_STEP
CB_ROWS = CB_BACK + CB_NEAR
CB_OFFSET = CMP_STRIDE * CB_BACK - (CMP_BLOCK - 1)
LOG2E = math.log2(math.e)
V_ROWS = HEAD_DIM + 16
VT_ROWS = N_KV_A * V_ROWS
LANES = 128
COUNT_ROWS = 32
VMEM_LIMIT = 56 * 1024 * 1024
INT_MIN = -(2 ** 31)

F32 = jnp.float32
BF16 = jnp.bfloat16
NT_DIMS = (((1,), (1,)), ((), ()))

ROW_ORDER = ("c_kv", "z_a", "z_b", "k_cmp", "v_cmp", "k_slc", "k_win", "k_idx")
ROW_WIDTH = 1792
COL_ORDER = ("q_a", "q_idx", "q_b", "v_slc", "v_win", "g_b", "w_idx")


def _layout(order):
    offs, o = {}, 0
    for n in order:
        w = PROJ_COLS[n][1] - PROJ_COLS[n][0]
        offs[n] = (o, o + w)
        o += w
    return offs, o


ROW_OFFS, _ROW_USED = _layout(ROW_ORDER)
COL_OFFS, COL_ROWS = _layout(COL_ORDER)


def _dot(a, b):
    return jnp.dot(a, b, preferred_element_type=F32)


def _dot_nt(a, b):
    return lax.dot_general(a, b, NT_DIMS, preferred_element_type=F32)


def _rms_rows(a, gain):
    ms = jnp.mean(a * a, axis=-1, keepdims=True)
    return a * lax.rsqrt(ms + EPS) * gain


def _group_rms_rows(a, gain):
    lane = lax.broadcasted_iota(jnp.int32, a.shape, 1)
    sq = a * a
    lo = lane < HEAD_DIM
    s_lo = jnp.sum(jnp.where(lo, sq, 0.0), axis=-1, keepdims=True)
    s_hi = jnp.sum(jnp.where(lo, 0.0, sq), axis=-1, keepdims=True)
    ms = jnp.where(lo, s_lo, s_hi) * (1.0 / HEAD_DIM)
    return a * lax.rsqrt(ms + EPS) * gain


def _proj_kernel(x_ref, g_ref, wrow_ref, wcol_ref, kvn_ref, wup_ref, wupv_ref, kna_ref, kns_ref, knw_ref,
                 qna_ref, qnb_ref, bg_ref,
                 ka_ref, kidx_ref, ks_ref, kw_ref, kcmp_ref, vcmp_ref, sz_ref,
                 qa_ref, qi_ref, qb_ref, va_ref, vs_ref, vw_ref, gt_ref, wt_ref):
    t = x_ref.shape[0]
    h = _rms_rows(x_ref[...], g_ref[...]).astype(BF16)
    pr = _dot(h, wrow_ref[...])

    def rows(name):
        a, b = ROW_OFFS[name]
        return pr[:, a:b]

    one_row = jnp.where(lax.broadcasted_iota(jnp.int32, (V_ROWS - HEAD_DIM, t), 0) == 0, 1.0, 0.0).astype(BF16)

    def with_ones(vt):
        vt = vt.astype(BF16)
        return jnp.concatenate([vt[:HEAD_DIM], one_row, vt[HEAD_DIM:], one_row], axis=0)

    ckvn = _rms_rows(rows("c_kv"), kvn_ref[...]).astype(BF16)
    k_up = _dot(ckvn, wup_ref[...])
    ka_ref[...] = _group_rms_rows(k_up, kna_ref[...]).astype(BF16)
    va_ref[...] = with_ones(_dot_nt(wupv_ref[...], ckvn))
    za, zb = ROW_OFFS["z_a"][0], ROW_OFFS["z_b"][1]
    z = pr[:, za:zb]
    sz_ref[...] = z * jax.nn.sigmoid(z)
    kcmp_ref[...] = rows("k_cmp")
    vcmp_ref[...] = rows("v_cmp")
    ks_ref[...] = _group_rms_rows(rows("k_slc"), kns_ref[...]).astype(BF16)
    kw_ref[...] = _group_rms_rows(rows("k_win"), knw_ref[...]).astype(BF16)
    kidx_ref[...] = rows("k_idx").astype(BF16)

    pt = _dot_nt(wcol_ref[...], h)

    def cols(name):
        a, b = COL_OFFS[name]
        return pt[a:b, :]

    def qnorm_padded(q, gain_col):
        q = q.reshape(N_HEADS_A, HEAD_DIM, t)
        ms = jnp.mean(q * q, axis=1, keepdims=True)
        qn = (q * lax.rsqrt(ms + EPS) * gain_col[None] * (ATTN_SCALE * LOG2E)).astype(BF16)
        zeros = jnp.zeros((HEAD_DIM, t), BF16)
        parts = []
        for hh in range(N_HEADS_A):
            parts += [qn[hh], zeros] if hh < N_HEADS_A // N_KV_A else [zeros, qn[hh]]
        return jnp.concatenate(parts, axis=0)

    qa_ref[...] = qnorm_padded(cols("q_a"), qna_ref[...])
    qb_ref[...] = qnorm_padded(cols("q_b"), qnb_ref[...])
    qi_ref[...] = (cols("q_idx") * IDX_SCALE).astype(BF16)
    vs_ref[...] = with_ones(cols("v_slc"))
    vw_ref[...] = with_ones(cols("v_win"))
    gt_ref[...] = jax.nn.sigmoid(cols("g_b") + bg_ref[...])
    wt_ref[...] = cols("w_idx") * IDX_W_SCALE


def _const_spec(shape):
    nd = len(shape)
    return pl.BlockSpec(shape, lambda *_: (0,) * nd)


def _proj_call(x, g, wrow, wcol, kvn, wup, wupv, kna, kns, knw, qna, qnb, bg):
    b, l, _ = x.shape
    t = min(T_PROJ, l)
    grid = (b, l // t)
    row = lambda w: pl.BlockSpec((None, t, w), lambda i, j: (i, j, 0))
    col = lambda r: pl.BlockSpec((None, r, t), lambda i, j: (i, 0, j))
    rs = lambda w, dt: jax.ShapeDtypeStruct((b, l, w), dt)
    cs = lambda r, dt: jax.ShapeDtypeStruct((b, r, l), dt)
    consts = (g, wrow, wcol, kvn, wup, wupv, kna, kns, knw, qna, qnb, bg)
    return pl.pallas_call(
        _proj_kernel,
        grid=grid,
        in_specs=[row(D_MODEL)] + [_const_spec(c.shape) for c in consts],
        out_specs=[row(128), row(IDX_DIM), row(128), row(128), row(128), row(128), row(D_A + D_B),
                   col(2 * D_A), col(N_IDX_HEADS * IDX_DIM), col(2 * D_B), col(VT_ROWS), col(VT_ROWS), col(VT_ROWS),
                   col(N_HEADS_B * 3), col(N_IDX_HEADS)],
        out_shape=[rs(128, BF16), rs(IDX_DIM, BF16), rs(128, BF16), rs(128, BF16), rs(128, F32), rs(128, F32),
                   rs(D_A + D_B, F32),
                   cs(2 * D_A, BF16), cs(N_IDX_HEADS * IDX_DIM, BF16), cs(2 * D_B, BF16), cs(VT_ROWS, BF16),
                   cs(VT_ROWS, BF16), cs(VT_ROWS, BF16), cs(N_HEADS_B * 3, F32), cs(N_IDX_HEADS, F32)],
        compiler_params=pltpu.CompilerParams(dimension_semantics=("arbitrary", "arbitrary"),
                                             vmem_limit_bytes=VMEM_LIMIT),
        name="proj",
    )(x, *consts)


def _compress_kernel(kr_ref, vr_ref, pk_ref, pv_ref, wk1_ref, wv1_ref, wk2_ref, wv2_ref, knc_ref,
                     kc_ref, vct_ref):
    n = kr_ref.shape[0]
    row = lax.broadcasted_iota(jnp.int32, (n, 128), 0)
    valid = row < n - 1

    def branch(r_ref, p_ref, w1_ref, w2_ref):
        r = r_ref[...]
        a = _dot((r + p_ref[0:1, :]).astype(BF16), w1_ref[0])
        b = _dot((r + p_ref[1:2, :]).astype(BF16), w1_ref[1])
        pre = a + pltpu.roll(b, n - 1, axis=0)
        return _dot(jax.nn.gelu(pre).astype(BF16), w2_ref[...])

    kc = _group_rms_rows(branch(kr_ref, pk_ref, wk1_ref, wk2_ref), knc_ref[...])
    kc_ref[...] = jnp.where(valid, kc, 0.0).astype(BF16)
    vc = jnp.where(valid, branch(vr_ref, pv_ref, wv1_ref, wv2_ref), 0.0)
    vct_ref[...] = vc.T.astype(BF16)


def _compress_call(kr, vr, pk, pv, wk1, wv1, wk2, wv2, knc):
    b, n, w = kr.shape
    blk = pl.BlockSpec((None, n, w), lambda i: (i, 0, 0))
    consts = (pk, pv, wk1, wv1, wk2, wv2, knc)
    return pl.pallas_call(
        _compress_kernel,
        grid=(b,),
        in_specs=[blk, blk] + [_const_spec(c.shape) for c in consts],
        out_specs=[pl.BlockSpec((None, n, 128), lambda i: (i, 0, 0)),
                   pl.BlockSpec((None, 128, n), lambda i: (i, 0, 0))],
        out_shape=[jax.ShapeDtypeStruct((b, n, 128), BF16), jax.ShapeDtypeStruct((b, 128, n), BF16)],
        compiler_params=pltpu.CompilerParams(dimension_semantics=("arbitrary",), vmem_limit_bytes=VMEM_LIMIT),
        name="compress",
    )(kr, vr, *consts)


def _attend_tile(k_t, q_ref, vt_of, bias_of, shift_of, lg_ref, p_ref, m_ref, a_ref, acc_ref):
    heads = range(N_HEADS_A)
    for h in heads:
        lg_ref[h] = _dot(k_t, q_ref[128 * h:128 * (h + 1), :])
    for h in heads:
        shift = shift_of(h)
        for c in range(TQ // LANES):
            cs = slice(LANES * c, LANES * (c + 1))
            lg = lg_ref[h, :, cs]
            for term in bias_of(h, c):
                lg = lg + term
            m_old = m_ref[h, :, cs]
            m_new = jnp.maximum(m_old, jnp.max(lg, axis=0, keepdims=True) + shift)
            a_ref[h, :, cs] = jnp.exp2(m_old - m_new)
            m_ref[h, :, cs] = m_new
            p_ref[h, :, cs] = jnp.exp2(lg - (m_new - shift)).astype(BF16)
    for h in heads:
        acc_ref[h] = a_ref[h] * acc_ref[h] + _dot(vt_of(h), p_ref[h])


def _build_toeplitz(tab_ref, toe_ref):
    period = tab_ref.shape[-1]
    for h in range(toe_ref.shape[0]):
        rows = jnp.broadcast_to(tab_ref[h], (TS, period))
        toe_ref[h] = pltpu.roll(rows, 0, 1, stride=1, stride_axis=0)[:, :toe_ref.shape[-1]]


def _first_step():
    return jnp.logical_and(pl.program_id(0) == 0, pl.program_id(1) == 0)


def _reset(m_ref, acc_ref):
    m_ref[...] = jnp.full(m_ref.shape, NEG_INF, F32)
    acc_ref[...] = jnp.zeros(acc_ref.shape, F32)


def _normalized(acc_ref, h):
    return acc_ref[h, :HEAD_DIM, :] / acc_ref[h, HEAD_DIM:HEAD_DIM + 1, :]


def _store_transposed(o_ref, head_out):
    for c in range(N_HEADS_A // 2):
        pair = jnp.concatenate([head_out(2 * c), head_out(2 * c + 1)], axis=0)
        o_ref[:, 128 * c:128 * (c + 1)] = pair.T


def _dsa_kernel(far_ref, kidx_ref, ka_ref, vat_ref, qi_ref, qa_ref, wt_ref, tab_ref, o_ref,
                keys_ref, toe_ref, lg_ref, p_ref, m_ref, a_ref, acc_ref, *, k_top):
    @pl.when(_first_step())
    def _():
        _build_toeplitz(tab_ref, toe_ref)

    qi = pl.program_id(1)
    nk = qi + 1
    q0 = qi * TQ
    t_pos = q0 + lax.broadcasted_iota(jnp.int32, (TS, TQ), 1)
    s_off = lax.broadcasted_iota(jnp.int32, (TS, TQ), 0)

    def score_body(kt, carry):
        s0 = pl.multiple_of(kt * TS, TS)
        kx = kidx_ref[pl.ds(s0, TS), :]
        acc = jnp.zeros((TS, TQ), F32)
        for h in range(N_IDX_HEADS):
            d = _dot(kx, qi_ref[IDX_DIM * h:IDX_DIM * (h + 1), :])
            acc = acc + wt_ref[h:h + 1, :] * jnp.maximum(d, 0.0)
        acc = jnp.where(acc == 0.0, 0.0, acc)
        bits = pltpu.bitcast(acc, jnp.int32)
        key = bits ^ ((bits >> 31) & jnp.int32(0x7FFFFFFF))
        keys_ref[pl.ds(s0, TS), :] = jnp.where(s0 + s_off <= t_pos, key, INT_MIN)
        return carry

    lax.fori_loop(0, nk, score_body, 0)

    n_pairs = (nk + 1) // 2

    @pl.when((nk & 1) == 1)
    def _():
        keys_ref[pl.ds(pl.multiple_of(nk * TS, TS), TS), :] = jnp.full((TS, TQ), INT_MIN, jnp.int32)

    def count(pred):
        def body(kp, c):
            s0 = pl.multiple_of(kp * (2 * TS), 2 * TS)
            hit = jnp.where(pred(keys_ref[pl.ds(s0, 2 * TS), :]), 1, 0)
            return c + jnp.sum(hit.reshape(2 * TS // COUNT_ROWS, COUNT_ROWS, TQ), axis=0)
        partial = lax.fori_loop(0, n_pairs, body, jnp.zeros((COUNT_ROWS, TQ), jnp.int32))
        return jnp.sum(partial, axis=0, keepdims=True)

    def bisect(it, thr_u):
        cand_u = thr_u | lax.shift_left(jnp.int32(1), 31 - it)
        cand_s = cand_u ^ INT_MIN
        n_ge = count(lambda tile: tile >= cand_s)
        return jnp.where(n_ge >= k_top, cand_u, thr_u)

    thr_u = lax.fori_loop(0, 32, bisect, jnp.zeros((1, TQ), jnp.int32))
    thr = jnp.maximum(thr_u ^ INT_MIN, INT_MIN + 1)

    n_gt = count(lambda tile: tile > thr)
    n_ge = count(lambda tile: tile >= thr)
    tie = n_ge > k_top
    need = jnp.where(tie, (k_top - n_gt).astype(F32), float(2 ** 30))

    @pl.when(jnp.max(jnp.where(tie, 1, 0)) > 0)
    def _():
        lower = (lax.broadcasted_iota(jnp.int32, (TS, TS), 1)
                 <= lax.broadcasted_iota(jnp.int32, (TS, TS), 0))
        tri = jnp.where(lower, 1.0, 0.0).astype(BF16)

        def body(kt, run):
            s0 = pl.multiple_of(kt * TS, TS)
            tile = keys_ref[pl.ds(s0, TS), :]
            eq = tile == thr
            pref = _dot(tri, jnp.where(eq, 1.0, 0.0).astype(BF16))
            surplus = jnp.where(eq, run + pref, 0.0) > need
            keys_ref[pl.ds(s0, TS), :] = jnp.where(surplus, thr - 1, tile)
            return run + pref[TS - 1:TS, :]

        lax.fori_loop(0, nk, body, jnp.zeros((1, TQ), F32))

    _reset(m_ref, acc_ref)

    def att_tile(near, kt, carry):
        s0 = pl.multiple_of(kt * TS, TS)
        off = pl.multiple_of((qi - kt) * TQ, TQ)
        unselected = jnp.where(keys_ref[pl.ds(s0, TS), :] >= thr, 0.0, NEG_INF)
        unsel_of = lambda c: unselected[:, LANES * c:LANES * (c + 1)]
        if near:
            bias_of = lambda h, c: (toe_ref[h, :, pl.ds(pl.multiple_of(off + LANES * c, LANES), LANES)],
                                    unsel_of(c))
            shift_of = lambda h: 0.0
        else:
            bias_of = lambda h, c: (unsel_of(c),)
            shift_of = lambda h: far_ref[h]
        _attend_tile(ka_ref[pl.ds(s0, TS), :], qa_ref, _vt_of(vat_ref, s0), bias_of, shift_of,
                     lg_ref, p_ref, m_ref, a_ref, acc_ref)
        return carry

    n_far = jnp.maximum(qi - (FAR_TILES - 1), 0)
    lax.fori_loop(0, n_far, functools.partial(att_tile, False), 0)
    lax.fori_loop(n_far, nk, functools.partial(att_tile, True), 0)

    _store_transposed(o_ref, functools.partial(_normalized, acc_ref))


def _vt_of(v_ref, s0):
    heads_per_group = N_HEADS_A // N_KV_A
    s0 = pl.multiple_of(s0, TS)
    return lambda h: v_ref[V_ROWS * (h // heads_per_group):V_ROWS * (h // heads_per_group + 1), pl.ds(s0, TS)]


def _attn_scratch():
    return [pltpu.VMEM((N_HEADS_A, TS, TQ), F32),
            pltpu.VMEM((N_HEADS_A, TS, TQ), BF16),
            pltpu.VMEM((N_HEADS_A, 1, TQ), F32),
            pltpu.VMEM((N_HEADS_A, 1, TQ), F32),
            pltpu.VMEM((N_HEADS_A, V_ROWS, TQ), F32)]


def _dsa_call(far, kidx, ka, vat, qi_t, qa_t, wt, tab):
    b, l, _ = ka.shape
    nq = l // TQ
    assert (l // TS) % 2 == 0
    full_rows = lambda w: pl.BlockSpec((None, l, w), lambda i, j: (i, 0, 0))
    qcol = lambda r: pl.BlockSpec((None, r, TQ), lambda i, j: (i, 0, j))
    return pl.pallas_call(
        functools.partial(_dsa_kernel, k_top=min(TOPK_A, l // 4)),
        grid=(b, nq),
        in_specs=[pl.BlockSpec(memory_space=pltpu.SMEM),
                  full_rows(IDX_DIM), full_rows(128),
                  pl.BlockSpec((None, VT_ROWS, l), lambda i, j: (i, 0, 0)),
                  qcol(N_IDX_HEADS * IDX_DIM), qcol(2 * D_A), qcol(N_IDX_HEADS),
                  _const_spec(tab.shape)],
        out_specs=pl.BlockSpec((None, TQ, D_A), lambda i, j: (i, j, 0)),
        out_shape=jax.ShapeDtypeStruct((b, l, D_A), F32),
        scratch_shapes=[pltpu.VMEM((l, TQ), jnp.int32),
                        pltpu.VMEM((N_HEADS_A, TS, TOE_WIDTH), F32)] + _attn_scratch(),
        compiler_params=pltpu.CompilerParams(dimension_semantics=("arbitrary", "arbitrary"),
                                             vmem_limit_bytes=VMEM_LIMIT),
        name="dsa",
    )(far, kidx, ka, vat, qi_t, qa_t, wt, tab)


def _nsa_kernel(far_ref, kc_ref, vct_ref, ks_ref, vst_ref, kw_ref, vwt_ref, qb_ref, gt_ref,
                tab_ref, tabc_ref, slct_ref, o_ref,
                toe_ref, toew_ref, cbn_ref, lgc_ref, pc_ref, oc_ref, vals_ref, sel_ref, ow_ref,
                lg_ref, p_ref, m_ref, a_ref, acc_ref, *, n_sel):
    n_win = WINDOW // TS

    @pl.when(_first_step())
    def _():
        _build_toeplitz(tab_ref, toe_ref)
        inside = (lax.broadcasted_iota(jnp.int32, (TS, TQ), 1) < lax.broadcasted_iota(jnp.int32, (TS, TQ), 0))
        for h in range(N_HEADS_B):
            toew_ref[h] = jnp.where(inside, toe_ref[h, :, n_win * TQ:(n_win + 1) * TQ], NEG_INF)
            rows = jnp.broadcast_to(tabc_ref[h], (CB_ROWS, TOE_PERIOD))
            cbn_ref[h] = pltpu.roll(rows, 0, 1, stride=CMP_STRIDE, stride_axis=0)[:, :TQ]

    qi = pl.program_id(1)
    nk = qi + 1
    q0 = qi * TQ
    n_c = kc_ref.shape[0]
    n_sb = slct_ref.shape[0]
    heads_per_group = N_HEADS_B // N_KV_B
    cb_row0 = pl.multiple_of(jnp.maximum(qi * CB_STEP - CB_BACK, 0), CB_STEP)
    cb_tab0 = pl.multiple_of(CB_BACK - (qi * CB_STEP - cb_row0), CB_STEP)

    t_c = q0 + lax.broadcasted_iota(jnp.int32, (n_c, TQ), 1)
    end_c = lax.broadcasted_iota(jnp.int32, (n_c, TQ), 0) * CMP_STRIDE + (CMP_BLOCK - 1)
    mask_c = t_c >= end_c
    kc = kc_ref[...]
    slct = slct_ref[...]
    t_b = q0 + lax.broadcasted_iota(jnp.int32, (n_sb, TQ), 1)
    blk = lax.broadcasted_iota(jnp.int32, (n_sb, TQ), 0)
    jt = t_b >> int(math.log2(SLC_BLOCK))
    forced = (blk == 0) | (blk == jt) | (blk == jt - 1)
    for h in range(N_HEADS_B):
        lgc_ref[h] = _dot(kc, qb_ref[128 * h:128 * (h + 1), :]) + far_ref[h]
        lgc_ref[h, pl.ds(cb_row0, CB_NEAR), :] = (lgc_ref[h, pl.ds(cb_row0, CB_NEAR), :]
                                                  + cbn_ref[h, pl.ds(cb_tab0, CB_NEAR), :])
    for h in range(N_HEADS_B):
        lg = jnp.where(mask_c, lgc_ref[h], NEG_INF)
        e = jnp.where(mask_c, jnp.exp2(lg - jnp.max(lg, axis=0, keepdims=True)), 0.0)
        den = jnp.sum(e, axis=0, keepdims=True)
        p = e * jnp.where(den > 0.0, 1.0 / den, 0.0)
        p_hi = p.astype(BF16)
        pc_ref[0, h] = p_hi
        pc_ref[1, h] = (p - p_hi.astype(F32)).astype(BF16)
    for g in range(N_KV_B):
        imp = jnp.zeros((n_sb, TQ), F32)
        for r in range(heads_per_group):
            h = g * heads_per_group + r
            oc_ref[h] = _dot(vct_ref[HEAD_DIM * g:HEAD_DIM * (g + 1), :], pc_ref[0, h])
            imp = imp + _dot(slct, pc_ref[0, h]) + _dot(slct, pc_ref[1, h])
        vals = imp + jnp.where(forced, FORCE_BONUS, 0.0)
        vals = jnp.where(blk * SLC_BLOCK <= t_b, vals, NEG_INF)
        vals_ref[...] = vals
        row8 = lax.broadcasted_iota(jnp.int32, (8, TQ), 0)
        v_rows = [vals[8 * rg:8 * (rg + 1)] for rg in range(n_sb // 8)]
        ranks = [jnp.zeros((8, TQ), jnp.int32) for _ in v_rows]
        for m in range(n_sb):
            vm = jnp.broadcast_to(vals_ref[m:m + 1, :], (8, TQ))
            for rg, v in enumerate(v_rows):
                if 8 * rg > m:
                    ahead = jnp.where(vm >= v, 1, 0)
                elif 8 * rg + 7 <= m:
                    ahead = jnp.where(vm > v, 1, 0)
                else:
                    ahead = jnp.where(row8 > m - 8 * rg, jnp.where(vm >= v, 1, 0), jnp.where(vm > v, 1, 0))
                ranks[rg] = ranks[rg] + ahead
        for rg, rank in enumerate(ranks):
            sel_ref[g, 8 * rg:8 * (rg + 1), :] = jnp.where(rank < n_sel, 0.0, NEG_INF)

    _reset(m_ref, acc_ref)
    blocks_per_tile = TS // SLC_BLOCK

    def sel_tile(near, kt, carry):
        s0 = pl.multiple_of(kt * TS, TS)
        off = pl.multiple_of((qi - kt) * TQ, TQ)

        def unsel_of(h, c):
            g = h // heads_per_group
            return jnp.concatenate(
                [jnp.broadcast_to(sel_ref[g, pl.ds(kt * blocks_per_tile + i, 1), :][:, LANES * c:LANES * (c + 1)],
                                  (SLC_BLOCK, LANES)) for i in range(blocks_per_tile)], axis=0)

        if near:
            bias_of = lambda h, c: (toe_ref[h, :, pl.ds(pl.multiple_of(off + LANES * c, LANES), LANES)],
                                    unsel_of(h, c))
            shift_of = lambda h: 0.0
        else:
            bias_of = lambda h, c: (unsel_of(h, c),)
            shift_of = lambda h: far_ref[h]
        _attend_tile(ks_ref[pl.ds(s0, TS), :], qb_ref, _vt_of(vst_ref, s0), bias_of, shift_of,
                     lg_ref, p_ref, m_ref, a_ref, acc_ref)
        return carry

    n_far = jnp.maximum(qi - (FAR_TILES - 1), 0)
    lax.fori_loop(0, n_far, functools.partial(sel_tile, False), 0)
    lax.fori_loop(n_far, nk, functools.partial(sel_tile, True), 0)
    for h in range(N_HEADS_B):
        ow_ref[h] = (gt_ref[3 * h:3 * h + 1, :] * oc_ref[h]
                     + gt_ref[3 * h + 1:3 * h + 2, :] * _normalized(acc_ref, h))

    _reset(m_ref, acc_ref)

    def win_tile(back):
        s0 = pl.multiple_of((qi - back) * TS, TS)
        if back == n_win:
            bias_of = lambda h, c: (toew_ref[h, :, LANES * c:LANES * (c + 1)],)
        else:
            bias_of = lambda h, c: (toe_ref[h, :, back * TQ + LANES * c:back * TQ + LANES * (c + 1)],)
        _attend_tile(kw_ref[pl.ds(s0, TS), :], qb_ref, _vt_of(vwt_ref, s0), bias_of, lambda h: 0.0,
                     lg_ref, p_ref, m_ref, a_ref, acc_ref)

    for back in range(n_win, 0, -1):
        pl.when(qi >= back)(functools.partial(win_tile, back))
    win_tile(0)

    _store_transposed(o_ref, lambda h: ow_ref[h] + gt_ref[3 * h + 2:3 * h + 3, :] * _normalized(acc_ref, h))


def _nsa_call(far, kc, vct, ks, vst, kw, vwt, qb_t, gt, tab, tabc, slct):
    b, l, _ = ks.shape
    nq = l // TQ
    n_c = kc.shape[1]
    n_sb = slct.shape[0]
    full_rows = pl.BlockSpec((None, l, 128), lambda i, j: (i, 0, 0))
    full_cols = pl.BlockSpec((None, VT_ROWS, l), lambda i, j: (i, 0, 0))
    qcol = lambda r: pl.BlockSpec((None, r, TQ), lambda i, j: (i, 0, j))
    head_buf = lambda rows: pltpu.VMEM((N_HEADS_B, rows, TQ), F32)
    return pl.pallas_call(
        functools.partial(_nsa_kernel, n_sel=min(N_SLC, n_sb)),
        grid=(b, nq),
        in_specs=[pl.BlockSpec(memory_space=pltpu.SMEM),
                  pl.BlockSpec((None, n_c, 128), lambda i, j: (i, 0, 0)),
                  pl.BlockSpec((None, 128, n_c), lambda i, j: (i, 0, 0)),
                  full_rows, full_cols, full_rows, full_cols,
                  qcol(2 * D_B), qcol(N_HEADS_B * 3),
                  _const_spec(tab.shape), _const_spec(tabc.shape), _const_spec(slct.shape)],
        out_specs=pl.BlockSpec((None, TQ, D_B), lambda i, j: (i, j, 0)),
        out_shape=jax.ShapeDtypeStruct((b, l, D_B), F32),
        scratch_shapes=[pltpu.VMEM((N_HEADS_B, TS, TOE_WIDTH), F32),
                        head_buf(TS),
                        head_buf(CB_ROWS),
                        head_buf(n_c),
                        pltpu.VMEM((2, N_HEADS_B, n_c, TQ), BF16),
                        head_buf(HEAD_DIM),
                        pltpu.VMEM((n_sb, TQ), F32),
                        pltpu.VMEM((N_KV_B, n_sb, TQ), F32),
                        head_buf(HEAD_DIM)] + _attn_scratch(),
        compiler_params=pltpu.CompilerParams(dimension_semantics=("arbitrary", "arbitrary"),
                                             vmem_limit_bytes=VMEM_LIMIT),
        name="nsa",
    )(far, kc, vct, ks, vst, kw, vwt, qb_t, gt, tab, tabc, slct)


def _out_kernel(x_ref, oa_ref, ob_ref, sz_ref, wa_ref, wb_ref, y_ref):
    sz = sz_ref[...]
    ya = (oa_ref[...] * sz[:, :D_A]).astype(BF16)
    yb = (ob_ref[...] * sz[:, D_A:]).astype(BF16)
    y_ref[...] = x_ref[...] + _dot(ya, wa_ref[...]) + _dot(yb, wb_ref[...])


def _out_call(x, oa, ob, sz, wa, wb):
    b, l, d = x.shape
    t = min(T_OUT, l)
    row = lambda w: pl.BlockSpec((None, t, w), lambda i, j: (i, j, 0))
    return pl.pallas_call(
        _out_kernel,
        grid=(b, l // t),
        in_specs=[row(d), row(D_A), row(D_B), row(D_A + D_B), _const_spec(wa.shape), _const_spec(wb.shape)],
        out_specs=row(d),
        out_shape=jax.ShapeDtypeStruct((b, l, d), F32),
        compiler_params=pltpu.CompilerParams(dimension_semantics=("arbitrary", "arbitrary"),
                                             vmem_limit_bytes=VMEM_LIMIT),
        name="out",
    )(x, oa, ob, sz, wa, wb)


def _t5_bucket(dist):
    n = jnp.maximum(dist, 0)
    max_exact = N_BUCKETS // 2
    nf = jnp.maximum(n, 1).astype(jnp.float32)
    log_b = max_exact + (jnp.log(nf / max_exact) / math.log(MAX_DISTANCE / max_exact)
                         * (N_BUCKETS - max_exact)).astype(jnp.int32)
    return jnp.where(n < max_exact, n, jnp.minimum(log_b, N_BUCKETS - 1))


def _bias_vectors(rel_bias):
    assert TOE_PERIOD >= TQ + CB_OFFSET and WINDOW % TS == 0 and WINDOW // TS < FAR_TILES
    k = np.arange(TOE_PERIOD)
    by_dist = rel_bias[_t5_bucket(jnp.arange(TOE_PERIOD))].T.astype(F32) * LOG2E
    far = rel_bias[N_BUCKETS - 1].astype(F32) * LOG2E
    tab = jnp.where(jnp.asarray(k < TOE_WIDTH)[None, :], by_dist, NEG_INF)
    dist_c = np.where(k < TQ, k, k - TOE_PERIOD) + CB_OFFSET
    tabc = jnp.where(jnp.asarray(dist_c >= 0)[None, :],
                     by_dist[:, np.clip(dist_c, 0, TOE_PERIOD - 1)] - far[:, None], 0.0)
    head_major = lambda a: a[N_HEADS_A:, None, :]
    return tab[:N_HEADS_A, None, :], head_major(tab), head_major(tabc), far[:N_HEADS_A], far[N_HEADS_A:]


def _overlap_map_t(n_c, n_sb):
    c_start = np.arange(n_c)[None, :] * CMP_STRIDE
    b_start = np.arange(n_sb)[:, None] * SLC_BLOCK
    ov = np.minimum(c_start + CMP_BLOCK, b_start + SLC_BLOCK) - np.maximum(c_start, b_start)
    return (np.maximum(ov, 0) / CMP_BLOCK).astype(np.float32)


def _pick(w, names, pad_to=None):
    parts = [w[:, PROJ_COLS[n][0]:PROJ_COLS[n][1]] for n in names]
    out = jnp.concatenate(parts, axis=1)
    if pad_to is not None and out.shape[1] < pad_to:
        out = jnp.pad(out, ((0, 0), (0, pad_to - out.shape[1])))
    return out


def _compress_weights(w1, w2, pos):
    half = CMP_BLOCK // 2
    eye = jnp.eye(N_KV_B, dtype=w1.dtype)
    w1r = w1.reshape(2, half, HEAD_DIM, HEAD_DIM)
    w1b = jnp.einsum("sjde,gk->sjgdke", w1r, eye).reshape(2, half * N_KV_B * HEAD_DIM, N_KV_B * HEAD_DIM)
    w2b = jnp.einsum("de,gk->gdke", w2, eye).reshape(N_KV_B * HEAD_DIM, N_KV_B * HEAD_DIM)
    posr = jnp.broadcast_to(pos.reshape(2, half, 1, HEAD_DIM), (2, half, N_KV_B, HEAD_DIM)).reshape(2, -1)
    return w1b.astype(BF16), w2b.astype(BF16), posr.astype(F32)


def kernel(x, rel_bias, norm_g, w_in, b_gate_b, kv_norm_a, w_kv_up_a, q_norm_a, k_norm_a, q_norm_b, k_norm_cmp,
           k_norm_slc, k_norm_win, cmp_pos_k, cmp_pos_v, w_cmp_k1, w_cmp_k2, w_cmp_v1, w_cmp_v2, w_out):
    b, l, d = x.shape
    depth = norm_g.shape[0]
    assert d == D_MODEL and l % T_PROJ == 0 and l % TQ == 0 and l // CMP_STRIDE >= CB_NEAR
    n_c = l // CMP_STRIDE
    n_sb = l // SLC_BLOCK
    tab_a, tab_b, tabc_b, far_a, far_b = _bias_vectors(rel_bias)
    slct = jnp.asarray(_overlap_map_t(n_c, n_sb)).astype(BF16)
    slct = slct * jnp.asarray(np.arange(n_c)[None, :] < n_c - 1, BF16)
    two = lambda v: jnp.tile(v, N_KV_B)[None, :].astype(F32)
    colv = lambda v: v[:, None].astype(F32)
    for layer in range(depth):
        w = w_in[layer]
        wrow = _pick(w, ROW_ORDER, ROW_WIDTH).astype(BF16)
        wcol = _pick(w, COL_ORDER).T.astype(BF16)
        wup = w_kv_up_a[layer]
        n_k = N_KV_A * HEAD_DIM
        (ka, kidx, ks, kw, kcmp, vcmp, sz, qa_t, qi_t, qb_t, va_t, vs_t, vw_t, g_t, w_t) = _proj_call(
            x, norm_g[layer][None, :], wrow, wcol, kv_norm_a[layer][None, :],
            wup[:, :n_k].astype(BF16), wup[:, n_k:].T.astype(BF16),
            two(k_norm_a[layer]), two(k_norm_slc[layer]), two(k_norm_win[layer]),
            colv(q_norm_a[layer]), colv(q_norm_b[layer]), colv(b_gate_b[layer]))
        wk1, wk2, pk = _compress_weights(w_cmp_k1[layer], w_cmp_k2[layer], cmp_pos_k[layer])
        wv1, wv2, pv = _compress_weights(w_cmp_v1[layer], w_cmp_v2[layer], cmp_pos_v[layer])
        row16 = CMP_STRIDE * N_KV_B * HEAD_DIM
        kc, vc_t = _compress_call(kcmp.reshape(b, n_c, row16), vcmp.reshape(b, n_c, row16),
                                  pk, pv, wk1, wv1, wk2, wv2, two(k_norm_cmp[layer]))
        o_a = _dsa_call(far_a, kidx, ka, va_t, qi_t, qa_t, w_t, tab_a)
        o_b = _nsa_call(far_b, kc, vc_t, ks, vs_t, kw, vw_t, qb_t, g_t, tab_b, tabc_b, slct)
        wo = w_out[layer].astype(BF16)
        x = _out_call(x, o_a, o_b, sz, wo[:D_A], wo[D_A:])
    return x
```

```python
import functools
import math

import jax
import jax.numpy as jnp
import numpy as np
from jax import lax
from jax.experimental import pallas as pl
from jax.experimental.pallas import tpu as pltpu

D_MODEL = 1024
HEAD_DIM = 64
N_HEADS_A = 8
N_HEADS_B = 8
D_A = N_HEADS_A * HEAD_DIM
D_B = N_HEADS_B * HEAD_DIM
KV_RANK_A = 128
N_KV_A = 2
N_IDX_HEADS = 8
IDX_DIM = 64
TOPK_A = 256
N_KV_B = 2
D_KV_B = N_KV_B * HEAD_DIM
CMP_BLOCK = 32
CMP_STRIDE = 16
SLC_BLOCK = 64
N_SLC = 16
WINDOW = 512
N_BUCKETS = 32
MAX_DISTANCE = 1024
EPS = 1e-6
NEG_INF = -1e30
FORCE_BONUS = 1e4
ATTN_SCALE = HEAD_DIM ** -0.5
IDX_SCALE = IDX_DIM ** -0.5
IDX_W_SCALE = N_IDX_HEADS ** -0.5

PROJ_SIZES = (
    D_A, KV_RANK_A, N_IDX_HEADS * IDX_DIM, IDX_DIM, N_IDX_HEADS, D_A,
    D_B, D_KV_B, D_KV_B, D_KV_B, D_KV_B, D_KV_B, D_KV_B, N_HEADS_B * 3, D_B,
)
PROJ_NAMES = ("q_a", "c_kv", "q_idx", "k_idx", "w_idx", "z_a",
              "q_b", "k_cmp", "v_cmp", "k_slc", "v_slc", "k_win", "v_win", "g_b", "z_b")
_OFFS = np.concatenate([[0], np.cumsum(PROJ_SIZES)])
PROJ_COLS = {n: (int(_OFFS[i]), int(_OFFS[i + 1])) for i, n in enumerate(PROJ_NAMES)}

TQ = 256
TS = 256
T_PROJ = 512
T_OUT = 512
FAR_DIST = MAX_DISTANCE + 1
FAR_TILES = -(-(FAR_DIST + TS - 1) // TQ)
TOE_WIDTH = FAR_TILES * TQ
TOE_PERIOD = TOE_WIDTH + TS
CB_STEP = TQ // CMP_STRIDE
CB_BACK = -(-(FAR_DIST + CMP_BLOCK - 1 - CMP_STRIDE) // (CMP_STRIDE * CB_STEP)) * CB_STEP
CB_NEAR = CB_BACK + CB_STEP
CB_ROWS = CB_BACK + CB_NEAR
CB_OFFSET = CMP_STRIDE * CB_BACK - (CMP_BLOCK - 1)
LOG2E = math.log2(math.e)
V_ROWS = HEAD_DIM + 16
VT_ROWS = N_KV_A * V_ROWS
LANES = 128
COUNT_ROWS = 32
VMEM_LIMIT = 56 * 1024 * 1024
INT_MIN = -(2 ** 31)

F32 = jnp.float32
BF16 = jnp.bfloat16
NT_DIMS = (((1,), (1,)), ((), ()))

ROW_ORDER = ("c_kv", "z_a", "z_b", "k_cmp", "v_cmp", "k_slc", "k_win", "k_idx")
ROW_WIDTH = 1792
COL_ORDER = ("q_a", "q_idx", "q_b", "v_slc", "v_win", "g_b", "w_idx")


def _layout(order):
    offs, o = {}, 0
    for n in order:
        w = PROJ_COLS[n][1] - PROJ_COLS[n][0]
        offs[n] = (o, o + w)
        o += w
    return offs, o


ROW_OFFS, _ROW_USED = _layout(ROW_ORDER)
COL_OFFS, COL_ROWS = _layout(COL_ORDER)


def _dot(a, b):
    return jnp.dot(a, b, preferred_element_type=F32)


def _dot_nt(a, b):
    return lax.dot_general(a, b, NT_DIMS, preferred_element_type=F32)


def _rms_rows(a, gain):
    ms = jnp.mean(a * a, axis=-1, keepdims=True)
    return a * lax.rsqrt(ms + EPS) * gain


def _group_rms_rows(a, gain):
    lane = lax.broadcasted_iota(jnp.int32, a.shape, 1)
    sq = a * a
    lo = lane < HEAD_DIM
    s_lo = jnp.sum(jnp.where(lo, sq, 0.0), axis=-1, keepdims=True)
    s_hi = jnp.sum(jnp.where(lo, 0.0, sq), axis=-1, keepdims=True)
    ms = jnp.where(lo, s_lo, s_hi) * (1.0 / HEAD_DIM)
    return a * lax.rsqrt(ms + EPS) * gain


def _proj_kernel(x_ref, g_ref, wrow_ref, wcol_ref, kvn_ref, wup_ref, wupv_ref, kna_ref, kns_ref, knw_ref,
                 qna_ref, qnb_ref, bg_ref,
                 ka_ref, kidx_ref, ks_ref, kw_ref, kcmp_ref, vcmp_ref, sz_ref,
                 qa_ref, qi_ref, qb_ref, va_ref, vs_ref, vw_ref, gt_ref, wt_ref):
    t = x_ref.shape[0]
    h = _rms_rows(x_ref[...], g_ref[...]).astype(BF16)
    pr = _dot(h, wrow_ref[...])

    def rows(name):
        a, b = ROW_OFFS[name]
        return pr[:, a:b]

    one_row = jnp.where(lax.broadcasted_iota(jnp.int32, (V_ROWS - HEAD_DIM, t), 0) == 0, 1.0, 0.0).astype(BF16)

    def with_ones(vt):
        vt = vt.astype(BF16)
        return jnp.concatenate([vt[:HEAD_DIM], one_row, vt[HEAD_DIM:], one_row], axis=0)

    ckvn = _rms_rows(rows("c_kv"), kvn_ref[...]).astype(BF16)
    k_up = _dot(ckvn, wup_ref[...])
    ka_ref[...] = _group_rms_rows(k_up, kna_ref[...]).astype(BF16)
    va_ref[...] = with_ones(_dot_nt(wupv_ref[...], ckvn))
    za, zb = ROW_OFFS["z_a"][0], ROW_OFFS["z_b"][1]
    z = pr[:, za:zb]
    sz_ref[...] = z * jax.nn.sigmoid(z)
    kcmp_ref[...] = rows("k_cmp")
    vcmp_ref[...] = rows("v_cmp")
    ks_ref[...] = _group_rms_rows(rows("k_slc"), kns_ref[...]).astype(BF16)
    kw_ref[...] = _group_rms_rows(rows("k_win"), knw_ref[...]).astype(BF16)
    kidx_ref[...] = rows("k_idx").astype(BF16)

    pt = _dot_nt(wcol_ref[...], h)

    def cols(name):
        a, b = COL_OFFS[name]
        return pt[a:b, :]

    def qnorm_padded(q, gain_col):
        q = q.reshape(N_HEADS_A, HEAD_DIM, t)
        ms = jnp.mean(q * q, axis=1, keepdims=True)
        qn = (q * lax.rsqrt(ms + EPS) * gain_col[None] * (ATTN_SCALE * LOG2E)).astype(BF16)
        zeros = jnp.zeros((HEAD_DIM, t), BF16)
        parts = []
        for hh in range(N_HEADS_A):
            parts += [qn[hh], zeros] if hh < N_HEADS_A // N_KV_A else [zeros, qn[hh]]
        return jnp.concatenate(parts, axis=0)

    qa_ref[...] = qnorm_padded(cols("q_a"), qna_ref[...])
    qb_ref[...] = qnorm_padded(cols("q_b"), qnb_ref[...])
    qi_ref[...] = (cols("q_idx") * IDX_SCALE).astype(BF16)
    vs_ref[...] = with_ones(cols("v_slc"))
    vw_ref[...] = with_ones(cols("v_win"))
    gt_ref[...] = jax.nn.sigmoid(cols("g_b") + bg_ref[...])
    wt_ref[...] = cols("w_idx") * IDX_W_SCALE


def _const_spec(shape):
    nd = len(shape)
    return pl.BlockSpec(shape, lambda *_: (0,) * nd)


def _proj_call(x, g, wrow, wcol, kvn, wup, wupv, kna, kns, knw, qna, qnb, bg):
    b, l, _ = x.shape
    t = min(T_PROJ, l)
    grid = (b, l // t)
    row = lambda w: pl.BlockSpec((None, t, w), lambda i, j: (i, j, 0))
    col = lambda r: pl.BlockSpec((None, r, t), lambda i, j: (i, 0, j))
    rs = lambda w, dt: jax.ShapeDtypeStruct((b, l, w), dt)
    cs = lambda r, dt: jax.ShapeDtypeStruct((b, r, l), dt)
    consts = (g, wrow, wcol, kvn, wup, wupv, kna, kns, knw, qna, qnb, bg)
    return pl.pallas_call(
        _proj_kernel,
        grid=grid,
        in_specs=[row(D_MODEL)] + [_const_spec(c.shape) for c in consts],
        out_specs=[row(128), row(IDX_DIM), row(128), row(128), row(128), row(128), row(D_A + D_B),
                   col(2 * D_A), col(N_IDX_HEADS * IDX_DIM), col(2 * D_B), col(VT_ROWS), col(VT_ROWS), col(VT_ROWS),
                   col(N_HEADS_B * 3), col(N_IDX_HEADS)],
        out_shape=[rs(128, BF16), rs(IDX_DIM, BF16), rs(128, BF16), rs(128, BF16), rs(128, F32), rs(128, F32),
                   rs(D_A + D_B, F32),
                   cs(2 * D_A, BF16), cs(N_IDX_HEADS * IDX_DIM, BF16), cs(2 * D_B, BF16), cs(VT_ROWS, BF16),
                   cs(VT_ROWS, BF16), cs(VT_ROWS, BF16), cs(N_HEADS_B * 3, F32), cs(N_IDX_HEADS, F32)],
        compiler_params=pltpu.CompilerParams(dimension_semantics=("arbitrary", "arbitrary"),
                                             vmem_limit_bytes=VMEM_LIMIT),
        name="proj",
    )(x, *consts)


def _compress_kernel(kr_ref, vr_ref, pk_ref, pv_ref, wk1_ref, wv1_ref, wk2_ref, wv2_ref, knc_ref,
                     kc_ref, vct_ref):
    n = kr_ref.shape[0]
    row = lax.broadcasted_iota(jnp.int32, (n, 128), 0)
    valid = row < n - 1

    def branch(r_ref, p_ref, w1_ref, w2_ref):
        r = r_ref[...]
        a = _dot((r + p_ref[0:1, :]).astype(BF16), w1_ref[0])
        b = _dot((r + p_ref[1:2, :]).astype(BF16), w1_ref[1])
        pre = a + pltpu.roll(b, n - 1, axis=0)
        return _dot(jax.nn.gelu(pre).astype(BF16), w2_ref[...])

    kc = _group_rms_rows(branch(kr_ref, pk_ref, wk1_ref, wk2_ref), knc_ref[...])
    kc_ref[...] = jnp.where(valid, kc, 0.0).astype(BF16)
    vc = jnp.where(valid, branch(vr_ref, pv_ref, wv1_ref, wv2_ref), 0.0)
    vct_ref[...] = vc.T.astype(BF16)


def _compress_call(kr, vr, pk, pv, wk1, wv1, wk2, wv2, knc):
    b, n, w = kr.shape
    blk = pl.BlockSpec((None, n, w), lambda i: (i, 0, 0))
    consts = (pk, pv, wk1, wv1, wk2, wv2, knc)
    return pl.pallas_call(
        _compress_kernel,
        grid=(b,),
        in_specs=[blk, blk] + [_const_spec(c.shape) for c in consts],
        out_specs=[pl.BlockSpec((None, n, 128), lambda i: (i, 0, 0)),
                   pl.BlockSpec((None, 128, n), lambda i: (i, 0, 0))],
        out_shape=[jax.ShapeDtypeStruct((b, n, 128), BF16), jax.ShapeDtypeStruct((b, 128, n), BF16)],
        compiler_params=pltpu.CompilerParams(dimension_semantics=("arbitrary",), vmem_limit_bytes=VMEM_LIMIT),
        name="compress",
    )(kr, vr, *consts)


def _flush_tile(vt_prev, p_ref, a_ref, acc_ref, h):
    acc_ref[h] = a_ref[h] * acc_ref[h] + _dot(vt_prev(h), p_ref[h])


def _flush(vt_prev, p_ref, a_ref, acc_ref):
    for h in range(N_HEADS_A):
        _flush_tile(vt_prev, p_ref, a_ref, acc_ref, h)


def _attend_tile(k_t, q_ref, vt_prev, first, bias_of, shift_of, lg_ref, p_ref, m_ref, a_ref, acc_ref):
    heads = range(N_HEADS_A)
    for h in heads:
        lg_ref[h] = _dot(k_t, q_ref[128 * h:128 * (h + 1), :])
    for h in heads:
        _flush_tile(vt_prev, p_ref, a_ref, acc_ref, h)
    for h in heads:
        shift = shift_of(h)
        for c in range(TQ // LANES):
            cs = slice(LANES * c, LANES * (c + 1))
            lg = lg_ref[h, :, cs]
            for term in bias_of(h, c):
                lg = lg + term
            m_old = m_ref[h, :, cs]
            m_new = jnp.maximum(m_old, jnp.max(lg, axis=0, keepdims=True) + shift)
            a_ref[h, :, cs] = jnp.exp2(m_old - m_new)
            m_ref[h, :, cs] = m_new
            p_ref[h, :, cs] = jnp.exp2(lg - (m_new - shift)).astype(BF16)

    @pl.when(first)
    def _():
        acc_ref[...] = jnp.zeros(acc_ref.shape, F32)


def _build_toeplitz(tab_ref, toe_ref):
    period = tab_ref.shape[-1]
    for h in range(toe_ref.shape[0]):
        rows = jnp.broadcast_to(tab_ref[h], (TS, period))
        toe_ref[h] = pltpu.roll(rows, 0, 1, stride=1, stride_axis=0)[:, :toe_ref.shape[-1]]


def _first_step():
    return jnp.logical_and(pl.program_id(0) == 0, pl.program_id(1) == 0)


def _reset(m_ref):
    m_ref[...] = jnp.full(m_ref.shape, NEG_INF, F32)


def _init_staging(p_ref, a_ref, acc_ref):
    p_ref[...] = jnp.zeros(p_ref.shape, BF16)
    a_ref[...] = jnp.ones(a_ref.shape, F32)
    acc_ref[...] = jnp.zeros(acc_ref.shape, F32)


def _normalized(acc_ref, h):
    return acc_ref[h, :HEAD_DIM, :] / acc_ref[h, HEAD_DIM:HEAD_DIM + 1, :]


def _store_transposed(o_ref, head_out):
    for c in range(N_HEADS_A // 2):
        pair = jnp.concatenate([head_out(2 * c), head_out(2 * c + 1)], axis=0)
        o_ref[:, 128 * c:128 * (c + 1)] = pair.T


def _dsa_kernel(far_ref, kidx_ref, ka_ref, vat_ref, qi_ref, qa_ref, wt_ref, tab_ref, o_ref,
                keys_ref, toe_ref, lg_ref, p_ref, m_ref, a_ref, acc_ref, *, k_top):
    @pl.when(_first_step())
    def _():
        _build_toeplitz(tab_ref, toe_ref)
        _init_staging(p_ref, a_ref, acc_ref)

    qi = pl.program_id(1)
    nk = qi + 1
    q0 = qi * TQ
    t_pos = q0 + lax.broadcasted_iota(jnp.int32, (TS, TQ), 1)
    s_off = lax.broadcasted_iota(jnp.int32, (TS, TQ), 0)

    def score_body(kt, carry):
        s0 = pl.multiple_of(kt * TS, TS)
        kx = kidx_ref[pl.ds(s0, TS), :]
        acc = jnp.zeros((TS, TQ), F32)
        for h in range(N_IDX_HEADS):
            d = _dot(kx, qi_ref[IDX_DIM * h:IDX_DIM * (h + 1), :])
            acc = acc + wt_ref[h:h + 1, :] * jnp.maximum(d, 0.0)
        acc = jnp.where(acc == 0.0, 0.0, acc)
        bits = pltpu.bitcast(acc, jnp.int32)
        key = bits ^ ((bits >> 31) & jnp.int32(0x7FFFFFFF))
        keys_ref[pl.ds(s0, TS), :] = jnp.where(s0 + s_off <= t_pos, key, INT_MIN)
        return carry

    lax.fori_loop(0, nk, score_body, 0)

    n_pairs = (nk + 1) // 2

    @pl.when((nk & 1) == 1)
    def _():
        keys_ref[pl.ds(pl.multiple_of(nk * TS, TS), TS), :] = jnp.full((TS, TQ), INT_MIN, jnp.int32)

    def count(pred):
        def body(kp, c):
            s0 = pl.multiple_of(kp * (2 * TS), 2 * TS)
            hit = jnp.where(pred(keys_ref[pl.ds(s0, 2 * TS), :]), 1, 0)
            return c + jnp.sum(hit.reshape(2 * TS // COUNT_ROWS, COUNT_ROWS, TQ), axis=0)
        partial = lax.fori_loop(0, n_pairs, body, jnp.zeros((COUNT_ROWS, TQ), jnp.int32))
        return jnp.sum(partial, axis=0, keepdims=True)

    def bisect(it, thr_u):
        cand_u = thr_u | lax.shift_left(jnp.int32(1), 31 - it)
        cand_s = cand_u ^ INT_MIN
        n_ge = count(lambda tile: tile >= cand_s)
        return jnp.where(n_ge >= k_top, cand_u, thr_u)

    thr_u = lax.fori_loop(0, 32, bisect, jnp.zeros((1, TQ), jnp.int32))
    thr = jnp.maximum(thr_u ^ INT_MIN, INT_MIN + 1)

    n_gt = count(lambda tile: tile > thr)
    n_ge = count(lambda tile: tile >= thr)
    tie = n_ge > k_top
    need = jnp.where(tie, (k_top - n_gt).astype(F32), float(2 ** 30))

    @pl.when(jnp.max(jnp.where(tie, 1, 0)) > 0)
    def _():
        lower = (lax.broadcasted_iota(jnp.int32, (TS, TS), 1)
                 <= lax.broadcasted_iota(jnp.int32, (TS, TS), 0))
        tri = jnp.where(lower, 1.0, 0.0).astype(BF16)

        def body(kt, run):
            s0 = pl.multiple_of(kt * TS, TS)
            tile = keys_ref[pl.ds(s0, TS), :]
            eq = tile == thr
            pref = _dot(tri, jnp.where(eq, 1.0, 0.0).astype(BF16))
            surplus = jnp.where(eq, run + pref, 0.0) > need
            keys_ref[pl.ds(s0, TS), :] = jnp.where(surplus, thr - 1, tile)
            return run + pref[TS - 1:TS, :]

        lax.fori_loop(0, nk, body, jnp.zeros((1, TQ), F32))

    _reset(m_ref)

    def att_tile(near, kt, s_prev):
        s0 = pl.multiple_of(kt * TS, TS)
        off = pl.multiple_of((qi - kt) * TQ, TQ)
        unselected = jnp.where(keys_ref[pl.ds(s0, TS), :] >= thr, 0.0, NEG_INF)
        unsel_of = lambda c: unselected[:, LANES * c:LANES * (c + 1)]
        if near:
            bias_of = lambda h, c: (toe_ref[h, :, pl.ds(pl.multiple_of(off + LANES * c, LANES), LANES)],
                                    unsel_of(c))
            shift_of = lambda h: 0.0
        else:
            bias_of = lambda h, c: (unsel_of(c),)
            shift_of = lambda h: far_ref[h]
        _attend_tile(ka_ref[pl.ds(s0, TS), :], qa_ref, _vt_of(vat_ref, s_prev), kt == 0, bias_of, shift_of,
                     lg_ref, p_ref, m_ref, a_ref, acc_ref)
        return s0

    n_far = jnp.maximum(qi - (FAR_TILES - 1), 0)
    s_last = lax.fori_loop(0, n_far, functools.partial(att_tile, False), jnp.int32(0))
    s_last = lax.fori_loop(n_far, nk, functools.partial(att_tile, True), s_last)
    _flush(_vt_of(vat_ref, s_last), p_ref, a_ref, acc_ref)

    _store_transposed(o_ref, functools.partial(_normalized, acc_ref))


def _vt_of(v_ref, s0):
    heads_per_group = N_HEADS_A // N_KV_A
    s0 = pl.multiple_of(s0, TS)
    return lambda h: v_ref[V_ROWS * (h // heads_per_group):V_ROWS * (h // heads_per_group + 1), pl.ds(s0, TS)]


def _attn_scratch():
    return [pltpu.VMEM((N_HEADS_A, TS, TQ), F32),
            pltpu.VMEM((N_HEADS_A, TS, TQ), BF16),
            pltpu.VMEM((N_HEADS_A, 1, TQ), F32),
            pltpu.VMEM((N_HEADS_A, 1, TQ), F32),
            pltpu.VMEM((N_HEADS_A, V_ROWS, TQ), F32)]


def _dsa_call(far, kidx, ka, vat, qi_t, qa_t, wt, tab):
    b, l, _ = ka.shape
    nq = l // TQ
    assert (l // TS) % 2 == 0
    full_rows = lambda w: pl.BlockSpec((None, l, w), lambda i, j: (i, 0, 0))
    qcol = lambda r: pl.BlockSpec((None, r, TQ), lambda i, j: (i, 0, j))
    return pl.pallas_call(
        functools.partial(_dsa_kernel, k_top=min(TOPK_A, l // 4)),
        grid=(b, nq),
        in_specs=[pl.BlockSpec(memory_space=pltpu.SMEM),
                  full_rows(IDX_DIM), full_rows(128),
                  pl.BlockSpec((None, VT_ROWS, l), lambda i, j: (i, 0, 0)),
                  qcol(N_IDX_HEADS * IDX_DIM), qcol(2 * D_A), qcol(N_IDX_HEADS),
                  _const_spec(tab.shape)],
        out_specs=pl.BlockSpec((None, TQ, D_A), lambda i, j: (i, j, 0)),
        out_shape=jax.ShapeDtypeStruct((b, l, D_A), F32),
        scratch_shapes=[pltpu.VMEM((l, TQ), jnp.int32),
                        pltpu.VMEM((N_HEADS_A, TS, TOE_WIDTH), F32)] + _attn_scratch(),
        compiler_params=pltpu.CompilerParams(dimension_semantics=("arbitrary", "arbitrary"),
                                             vmem_limit_bytes=VMEM_LIMIT),
        name="dsa",
    )(far, kidx, ka, vat, qi_t, qa_t, wt, tab)


def _nsa_kernel(far_ref, kc_ref, vct_ref, ks_ref, vst_ref, kw_ref, vwt_ref, qb_ref, gt_ref,
                tab_ref, tabc_ref, slct_ref, o_ref,
                toe_ref, toew_ref, cbn_ref, lgc_ref, pc_ref, oc_ref, vals_ref, sel_ref, ow_ref,
                lg_ref, p_ref, m_ref, a_ref, acc_ref, *, n_sel):
    n_win = WINDOW // TS

    @pl.when(_first_step())
    def _():
        _build_toeplitz(tab_ref, toe_ref)
        _init_staging(p_ref, a_ref, acc_ref)
        inside = (lax.broadcasted_iota(jnp.int32, (TS, TQ), 1) < lax.broadcasted_iota(jnp.int32, (TS, TQ), 0))
        for h in range(N_HEADS_B):
            toew_ref[h] = jnp.where(inside, toe_ref[h, :, n_win * TQ:(n_win + 1) * TQ], NEG_INF)
            rows = jnp.broadcast_to(tabc_ref[h], (CB_ROWS, TOE_PERIOD))
            cbn_ref[h] = pltpu.roll(rows, 0, 1, stride=CMP_STRIDE, stride_axis=0)[:, :TQ]

    qi = pl.program_id(1)
    nk = qi + 1
    q0 = qi * TQ
    n_c = kc_ref.shape[0]
    n_sb = slct_ref.shape[0]
    heads_per_group = N_HEADS_B // N_KV_B
    cb_row0 = pl.multiple_of(jnp.maximum(qi * CB_STEP - CB_BACK, 0), CB_STEP)
    cb_tab0 = pl.multiple_of(CB_BACK - (qi * CB_STEP - cb_row0), CB_STEP)

    t_c = q0 + lax.broadcasted_iota(jnp.int32, (n_c, TQ), 1)
    end_c = lax.broadcasted_iota(jnp.int32, (n_c, TQ), 0) * CMP_STRIDE + (CMP_BLOCK - 1)
    mask_c = t_c >= end_c
    kc = kc_ref[...]
    slct = slct_ref[...]
    t_b = q0 + lax.broadcasted_iota(jnp.int32, (n_sb, TQ), 1)
    blk = lax.broadcasted_iota(jnp.int32, (n_sb, TQ), 0)
    jt = t_b >> int(math.log2(SLC_BLOCK))
    forced = (blk == 0) | (blk == jt) | (blk == jt - 1)
    for h in range(N_HEADS_B):
        lgc_ref[h] = _dot(kc, qb_ref[128 * h:128 * (h + 1), :]) + far_ref[h]
        lgc_ref[h, pl.ds(cb_row0, CB_NEAR), :] = (lgc_ref[h, pl.ds(cb_row0, CB_NEAR), :]
                                                  + cbn_ref[h, pl.ds(cb_tab0, CB_NEAR), :])
    for h in range(N_HEADS_B):
        lg = jnp.where(mask_c, lgc_ref[h], NEG_INF)
        e = jnp.where(mask_c, jnp.exp2(lg - jnp.max(lg, axis=0, keepdims=True)), 0.0)
        den = jnp.sum(e, axis=0, keepdims=True)
        p = e * jnp.where(den > 0.0, 1.0 / den, 0.0)
        p_hi = p.astype(BF16)
        pc_ref[0, h] = p_hi
        pc_ref[1, h] = (p - p_hi.astype(F32)).astype(BF16)
    for g in range(N_KV_B):
        imp = jnp.zeros((n_sb, TQ), F32)
        for r in range(heads_per_group):
            h = g * heads_per_group + r
            oc_ref[h] = _dot(vct_ref[HEAD_DIM * g:HEAD_DIM * (g + 1), :], pc_ref[0, h])
            imp = imp + _dot(slct, pc_ref[0, h]) + _dot(slct, pc_ref[1, h])
        vals = imp + jnp.where(forced, FORCE_BONUS, 0.0)
        vals = jnp.where(blk * SLC_BLOCK <= t_b, vals, NEG_INF)
        vals_ref[...] = vals
        row8 = lax.broadcasted_iota(jnp.int32, (8, TQ), 0)
        v_rows = [vals[8 * rg:8 * (rg + 1)] for rg in range(n_sb // 8)]
        ranks = [jnp.zeros((8, TQ), jnp.int32) for _ in v_rows]
        for m in range(n_sb):
            vm = jnp.broadcast_to(vals_ref[m:m + 1, :], (8, TQ))
            for rg, v in enumerate(v_rows):
                if 8 * rg > m:
                    ahead = jnp.where(vm >= v, 1, 0)
                elif 8 * rg + 7 <= m:
                    ahead = jnp.where(vm > v, 1, 0)
                else:
                    ahead = jnp.where(row8 > m - 8 * rg, jnp.where(vm >= v, 1, 0), jnp.where(vm > v, 1, 0))
                ranks[rg] = ranks[rg] + ahead
        for rg, rank in enumerate(ranks):
            sel_ref[g, 8 * rg:8 * (rg + 1), :] = jnp.where(rank < n_sel, 0.0, NEG_INF)

    _reset(m_ref)
    blocks_per_tile = TS // SLC_BLOCK

    def sel_tile(near, kt, s_prev):
        s0 = pl.multiple_of(kt * TS, TS)
        off = pl.multiple_of((qi - kt) * TQ, TQ)

        def unsel_of(h, c):
            g = h // heads_per_group
            return jnp.concatenate(
                [jnp.broadcast_to(sel_ref[g, pl.ds(kt * blocks_per_tile + i, 1), :][:, LANES * c:LANES * (c + 1)],
                                  (SLC_BLOCK, LANES)) for i in range(blocks_per_tile)], axis=0)

        if near:
            bias_of = lambda h, c: (toe_ref[h, :, pl.ds(pl.multiple_of(off + LANES * c, LANES), LANES)],
                                    unsel_of(h, c))
            shift_of = lambda h: 0.0
        else:
            bias_of = lambda h, c: (unsel_of(h, c),)
            shift_of = lambda h: far_ref[h]
        _attend_tile(ks_ref[pl.ds(s0, TS), :], qb_ref, _vt_of(vst_ref, s_prev), kt == 0, bias_of, shift_of,
                     lg_ref, p_ref, m_ref, a_ref, acc_ref)
        return s0

    n_far = jnp.maximum(qi - (FAR_TILES - 1), 0)
    s_last = lax.fori_loop(0, n_far, functools.partial(sel_tile, False), jnp.int32(0))
    s_last = lax.fori_loop(n_far, nk, functools.partial(sel_tile, True), s_last)
    _flush(_vt_of(vst_ref, s_last), p_ref, a_ref, acc_ref)
    for h in range(N_HEADS_B):
        ow_ref[h] = (gt_ref[3 * h:3 * h + 1, :] * oc_ref[h]
                     + gt_ref[3 * h + 1:3 * h + 2, :] * _normalized(acc_ref, h))

    _reset(m_ref)
    oldest = jnp.minimum(qi, n_win)

    def win_tile(back):
        s0 = pl.multiple_of((qi - back) * TS, TS)
        s_prev = jnp.maximum(qi - back - 1, 0) * TS
        if back == n_win:
            bias_of = lambda h, c: (toew_ref[h, :, LANES * c:LANES * (c + 1)],)
        else:
            bias_of = lambda h, c: (toe_ref[h, :, back * TQ + LANES * c:back * TQ + LANES * (c + 1)],)
        _attend_tile(kw_ref[pl.ds(s0, TS), :], qb_ref, _vt_of(vwt_ref, s_prev), oldest == back, bias_of,
                     lambda h: 0.0, lg_ref, p_ref, m_ref, a_ref, acc_ref)

    for back in range(n_win, 0, -1):
        pl.when(qi >= back)(functools.partial(win_tile, back))
    win_tile(0)
    _flush(_vt_of(vwt_ref, qi * TS), p_ref, a_ref, acc_ref)

    _store_transposed(o_ref, lambda h: ow_ref[h] + gt_ref[3 * h + 2:3 * h + 3, :] * _normalized(acc_ref, h))


def _nsa_call(far, kc, vct, ks, vst, kw, vwt, qb_t, gt, tab, tabc, slct):
    b, l, _ = ks.shape
    nq = l // TQ
    n_c = kc.shape[1]
    n_sb = slct.shape[0]
    full_rows = pl.BlockSpec((None, l, 128), lambda i, j: (i, 0, 0))
    full_cols = pl.BlockSpec((None, VT_ROWS, l), lambda i, j: (i, 0, 0))
    qcol = lambda r: pl.BlockSpec((None, r, TQ), lambda i, j: (i, 0, j))
    head_buf = lambda rows: pltpu.VMEM((N_HEADS_B, rows, TQ), F32)
    return pl.pallas_call(
        functools.partial(_nsa_kernel, n_sel=min(N_SLC, n_sb)),
        grid=(b, nq),
        in_specs=[pl.BlockSpec(memory_space=pltpu.SMEM),
                  pl.BlockSpec((None, n_c, 128), lambda i, j: (i, 0, 0)),
                  pl.BlockSpec((None, 128, n_c), lambda i, j: (i, 0, 0)),
                  full_rows, full_cols, full_rows, full_cols,
                  qcol(2 * D_B), qcol(N_HEADS_B * 3),
                  _const_spec(tab.shape), _const_spec(tabc.shape), _const_spec(slct.shape)],
        out_specs=pl.BlockSpec((None, TQ, D_B), lambda i, j: (i, j, 0)),
        out_shape=jax.ShapeDtypeStruct((b, l, D_B), F32),
        scratch_shapes=[pltpu.VMEM((N_HEADS_B, TS, TOE_WIDTH), F32),
                        head_buf(TS),
                        head_buf(CB_ROWS),
                        head_buf(n_c),
                        pltpu.VMEM((2, N_HEADS_B, n_c, TQ), BF16),
                        head_buf(HEAD_DIM),
                        pltpu.VMEM((n_sb, TQ), F32),
                        pltpu.VMEM((N_KV_B, n_sb, TQ), F32),
                        head_buf(HEAD_DIM)] + _attn_scratch(),
        compiler_params=pltpu.CompilerParams(dimension_semantics=("arbitrary", "arbitrary"),
                                             vmem_limit_bytes=VMEM_LIMIT),
        name="nsa",
    )(far, kc, vct, ks, vst, kw, vwt, qb_t, gt, tab, tabc, slct)


def _out_kernel(x_ref, oa_ref, ob_ref, sz_ref, wa_ref, wb_ref, y_ref):
    sz = sz_ref[...]
    ya = (oa_ref[...] * sz[:, :D_A]).astype(BF16)
    yb = (ob_ref[...] * sz[:, D_A:]).astype(BF16)
    y_ref[...] = x_ref[...] + _dot(ya, wa_ref[...]) + _dot(yb, wb_ref[...])


def _out_call(x, oa, ob, sz, wa, wb):
    b, l, d = x.shape
    t = min(T_OUT, l)
    row = lambda w: pl.BlockSpec((None, t, w), lambda i, j: (i, j, 0))
    return pl.pallas_call(
        _out_kernel,
        grid=(b, l // t),
        in_specs=[row(d), row(D_A), row(D_B), row(D_A + D_B), _const_spec(wa.shape), _const_spec(wb.shape)],
        out_specs=row(d),
        out_shape=jax.ShapeDtypeStruct((b, l, d), F32),
        compiler_params=pltpu.CompilerParams(dimension_semantics=("arbitrary", "arbitrary"),
                                             vmem_limit_bytes=VMEM_LIMIT),
        name="out",
    )(x, oa, ob, sz, wa, wb)


def _t5_bucket(dist):
    n = jnp.maximum(dist, 0)
    max_exact = N_BUCKETS // 2
    nf = jnp.maximum(n, 1).astype(jnp.float32)
    log_b = max_exact + (jnp.log(nf / max_exact) / math.log(MAX_DISTANCE / max_exact)
                         * (N_BUCKETS - max_exact)).astype(jnp.int32)
    return jnp.where(n < max_exact, n, jnp.minimum(log_b, N_BUCKETS - 1))


def _bias_vectors(rel_bias):
    assert TOE_PERIOD >= TQ + CB_OFFSET and WINDOW % TS == 0 and WINDOW // TS < FAR_TILES
    k = np.arange(TOE_PERIOD)
    by_dist = rel_bias[_t5_bucket(jnp.arange(TOE_PERIOD))].T.astype(F32) * LOG2E
    far = rel_bias[N_BUCKETS - 1].astype(F32) * LOG2E
    tab = jnp.where(jnp.asarray(k < TOE_WIDTH)[None, :], by_dist, NEG_INF)
    dist_c = np.where(k < TQ, k, k - TOE_PERIOD) + CB_OFFSET
    tabc = jnp.where(jnp.asarray(dist_c >= 0)[None, :],
                     by_dist[:, np.clip(dist_c, 0, TOE_PERIOD - 1)] - far[:, None], 0.0)
    head_major = lambda a: a[N_HEADS_A:, None, :]
    return tab[:N_HEADS_A, None, :], head_major(tab), head_major(tabc), far[:N_HEADS_A], far[N_HEADS_A:]


def _overlap_map_t(n_c, n_sb):
    c_start = np.arange(n_c)[None, :] * CMP_STRIDE
    b_start = np.arange(n_sb)[:, None] * SLC_BLOCK
    ov = np.minimum(c_start + CMP_BLOCK, b_start + SLC_BLOCK) - np.maximum(c_start, b_start)
    return (np.maximum(ov, 0) / CMP_BLOCK).astype(np.float32)


def _pick(w, names, pad_to=None):
    parts = [w[:, PROJ_COLS[n][0]:PROJ_COLS[n][1]] for n in names]
    out = jnp.concatenate(parts, axis=1)
    if pad_to is not None and out.shape[1] < pad_to:
        out = jnp.pad(out, ((0, 0), (0, pad_to - out.shape[1])))
    return out


def _compress_weights(w1, w2, pos):
    half = CMP_BLOCK // 2
    eye = jnp.eye(N_KV_B, dtype=w1.dtype)
    w1r = w1.reshape(2, half, HEAD_DIM, HEAD_DIM)
    w1b = jnp.einsum("sjde,gk->sjgdke", w1r, eye).reshape(2, half * N_KV_B * HEAD_DIM, N_KV_B * HEAD_DIM)
    w2b = jnp.einsum("de,gk->gdke", w2, eye).reshape(N_KV_B * HEAD_DIM, N_KV_B * HEAD_DIM)
    posr = jnp.broadcast_to(pos.reshape(2, half, 1, HEAD_DIM), (2, half, N_KV_B, HEAD_DIM)).reshape(2, -1)
    return w1b.astype(BF16), w2b.astype(BF16), posr.astype(F32)


def kernel(x, rel_bias, norm_g, w_in, b_gate_b, kv_norm_a, w_kv_up_a, q_norm_a, k_norm_a, q_norm_b, k_norm_cmp,
           k_norm_slc, k_norm_win, cmp_pos_k, cmp_pos_v, w_cmp_k1, w_cmp_k2, w_cmp_v1, w_cmp_v2, w_out):
    b, l, d = x.shape
    depth = norm_g.shape[0]
    assert d == D_MODEL and l % T_PROJ == 0 and l % TQ == 0 and l // CMP_STRIDE >= CB_NEAR
    n_c = l // CMP_STRIDE
    n_sb = l // SLC_BLOCK
    tab_a, tab_b, tabc_b, far_a, far_b = _bias_vectors(rel_bias)
    slct = jnp.asarray(_overlap_map_t(n_c, n_sb)).astype(BF16)
    slct = slct * jnp.asarray(np.arange(n_c)[None, :] < n_c - 1, BF16)
    two = lambda v: jnp.tile(v, N_KV_B)[None, :].astype(F32)
    colv = lambda v: v[:, None].astype(F32)
    for layer in range(depth):
        w = w_in[layer]
        wrow = _pick(w, ROW_ORDER, ROW_WIDTH).astype(BF16)
        wcol = _pick(w, COL_ORDER).T.astype(BF16)
        wup = w_kv_up_a[layer]
        n_k = N_KV_A * HEAD_DIM
        (ka, kidx, ks, kw, kcmp, vcmp, sz, qa_t, qi_t, qb_t, va_t, vs_t, vw_t, g_t, w_t) = _proj_call(
            x, norm_g[layer][None, :], wrow, wcol, kv_norm_a[layer][None, :],
            wup[:, :n_k].astype(BF16), wup[:, n_k:].T.astype(BF16),
            two(k_norm_a[layer]), two(k_norm_slc[layer]), two(k_norm_win[layer]),
            colv(q_norm_a[layer]), colv(q_norm_b[layer]), colv(b_gate_b[layer]))
        wk1, wk2, pk = _compress_weights(w_cmp_k1[layer], w_cmp_k2[layer], cmp_pos_k[layer])
        wv1, wv2, pv = _compress_weights(w_cmp_v1[layer], w_cmp_v2[layer], cmp_pos_v[layer])
        row16 = CMP_STRIDE * N_KV_B * HEAD_DIM
        kc, vc_t = _compress_call(kcmp.reshape(b, n_c, row16), vcmp.reshape(b, n_c, row16),
                                  pk, pv, wk1, wv1, wk2, wv2, two(k_norm_cmp[layer]))
        o_a = _dsa_call(far_a, kidx, ka, va_t, qi_t, qa_t, w_t, tab_a)
        o_b = _nsa_call(far_b, kc, vc_t, ks, vs_t, kw, vw_t, qb_t, g_t, tab_b, tabc_b, slct)
        wo = w_out[layer].astype(BF16)
        x = _out_call(x, o_a, o_b, sz, wo[:D_A], wo[D_A:])
    return x
```

```python
import functools
import math

import jax
import jax.numpy as jnp
import numpy as np
from jax import lax
from jax.experimental import pallas as pl
from jax.experimental.pallas import tpu as pltpu

D_MODEL = 1024
HEAD_DIM = 64
N_HEADS_A = 8
N_HEADS_B = 8
D_A = N_HEADS_A * HEAD_DIM
D_B = N_HEADS_B * HEAD_DIM
KV_RANK_A = 128
N_KV_A = 2
N_IDX_HEADS = 8
IDX_DIM = 64
TOPK_A = 256
N_KV_B = 2
D_KV_B = N_KV_B * HEAD_DIM
CMP_BLOCK = 32
CMP_STRIDE = 16
SLC_BLOCK = 64
N_SLC = 16
WINDOW = 512
N_BUCKETS = 32
MAX_DISTANCE = 1024
EPS = 1e-6
NEG_INF = -1e30
FORCE_BONUS = 1e4
ATTN_SCALE = HEAD_DIM ** -0.5
IDX_SCALE = IDX_DIM ** -0.5
IDX_W_SCALE = N_IDX_HEADS ** -0.5

PROJ_SIZES = (
    D_A, KV_RANK_A, N_IDX_HEADS * IDX_DIM, IDX_DIM, N_IDX_HEADS, D_A,
    D_B, D_KV_B, D_KV_B, D_KV_B, D_KV_B, D_KV_B, D_KV_B, N_HEADS_B * 3, D_B,
)
PROJ_NAMES = ("q_a", "c_kv", "q_idx", "k_idx", "w_idx", "z_a",
              "q_b", "k_cmp", "v_cmp", "k_slc", "v_slc", "k_win", "v_win", "g_b", "z_b")
_OFFS = np.concatenate([[0], np.cumsum(PROJ_SIZES)])
PROJ_COLS = {n: (int(_OFFS[i]), int(_OFFS[i + 1])) for i, n in enumerate(PROJ_NAMES)}

TQ = 256
TS = 256
T_PROJ = 512
T_OUT = 512
FAR_DIST = MAX_DISTANCE + 1
FAR_TILES = -(-(FAR_DIST + TS - 1) // TQ)
TOE_WIDTH = FAR_TILES * TQ
TOE_PERIOD = TOE_WIDTH + TS
CB_STEP = TQ // CMP_STRIDE
CB_BACK = -(-(FAR_DIST + CMP_BLOCK - 1 - CMP_STRIDE) // (CMP_STRIDE * CB_STEP)) * CB_STEP
CB_NEAR = CB_BACK + CB_STEP
CB_ROWS = CB_BACK + CB_NEAR
CB_OFFSET = CMP_STRIDE * CB_BACK - (CMP_BLOCK - 1)
LOG2E = math.log2(math.e)
V_ROWS = HEAD_DIM + 16
VT_ROWS = N_KV_A * V_ROWS
LANES = 128
COUNT_ROWS = 32
VMEM_LIMIT = 56 * 1024 * 1024
INT_MIN = -(2 ** 31)

F32 = jnp.float32
BF16 = jnp.bfloat16
NT_DIMS = (((1,), (1,)), ((), ()))

ROW_ORDER = ("c_kv", "z_a", "z_b", "k_cmp", "v_cmp", "k_slc", "k_win", "k_idx")
ROW_WIDTH = 1792
COL_ORDER = ("q_a", "q_idx", "q_b", "v_slc", "v_win", "g_b", "w_idx")


def _layout(order):
    offs, o = {}, 0
    for n in order:
        w = PROJ_COLS[n][1] - PROJ_COLS[n][0]
        offs[n] = (o, o + w)
        o += w
    return offs, o


ROW_OFFS, _ROW_USED = _layout(ROW_ORDER)
COL_OFFS, COL_ROWS = _layout(COL_ORDER)


def _dot(a, b):
    return jnp.dot(a, b, preferred_element_type=F32)


def _dot_nt(a, b):
    return lax.dot_general(a, b, NT_DIMS, preferred_element_type=F32)


def _rms_rows(a, gain):
    ms = jnp.mean(a * a, axis=-1, keepdims=True)
    return a * lax.rsqrt(ms + EPS) * gain


def _group_rms_rows(a, gain):
    lane = lax.broadcasted_iota(jnp.int32, a.shape, 1)
    sq = a * a
    lo = lane < HEAD_DIM
    s_lo = jnp.sum(jnp.where(lo, sq, 0.0), axis=-1, keepdims=True)
    s_hi = jnp.sum(jnp.where(lo, 0.0, sq), axis=-1, keepdims=True)
    ms = jnp.where(lo, s_lo, s_hi) * (1.0 / HEAD_DIM)
    return a * lax.rsqrt(ms + EPS) * gain


def _proj_kernel(x_ref, g_ref, wrow_ref, wcol_ref, kvn_ref, wup_ref, wupv_ref, kna_ref, kns_ref, knw_ref,
                 qna_ref, qnb_ref, bg_ref,
                 ka_ref, kidx_ref, ks_ref, kw_ref, kcmp_ref, vcmp_ref, sz_ref,
                 qa_ref, qi_ref, qb_ref, va_ref, vs_ref, vw_ref, gt_ref, wt_ref):
    t = x_ref.shape[0]
    h = _rms_rows(x_ref[...], g_ref[...]).astype(BF16)
    pr = _dot(h, wrow_ref[...])

    def rows(name):
        a, b = ROW_OFFS[name]
        return pr[:, a:b]

    one_row = jnp.where(lax.broadcasted_iota(jnp.int32, (V_ROWS - HEAD_DIM, t), 0) == 0, 1.0, 0.0).astype(BF16)

    def with_ones(vt):
        vt = vt.astype(BF16)
        return jnp.concatenate([vt[:HEAD_DIM], one_row, vt[HEAD_DIM:], one_row], axis=0)

    ckvn = _rms_rows(rows("c_kv"), kvn_ref[...]).astype(BF16)
    k_up = _dot(ckvn, wup_ref[...])
    ka_ref[...] = _group_rms_rows(k_up, kna_ref[...]).astype(BF16)
    va_ref[...] = with_ones(_dot_nt(wupv_ref[...], ckvn))
    za, zb = ROW_OFFS["z_a"][0], ROW_OFFS["z_b"][1]
    z = pr[:, za:zb]
    sz_ref[...] = z * jax.nn.sigmoid(z)
    kcmp_ref[...] = rows("k_cmp")
    vcmp_ref[...] = rows("v_cmp")
    ks_ref[...] = _group_rms_rows(rows("k_slc"), kns_ref[...]).astype(BF16)
    kw_ref[...] = _group_rms_rows(rows("k_win"), knw_ref[...]).astype(BF16)
    kidx_ref[...] = rows("k_idx").astype(BF16)

    pt = _dot_nt(wcol_ref[...], h)

    def cols(name):
        a, b = COL_OFFS[name]
        return pt[a:b, :]

    def qnorm_padded(q, gain_col):
        q = q.reshape(N_HEADS_A, HEAD_DIM, t)
        ms = jnp.mean(q * q, axis=1, keepdims=True)
        qn = (q * lax.rsqrt(ms + EPS) * gain_col[None] * (ATTN_SCALE * LOG2E)).astype(BF16)
        zeros = jnp.zeros((HEAD_DIM, t), BF16)
        parts = []
        for hh in range(N_HEADS_A):
            parts += [qn[hh], zeros] if hh < N_HEADS_A // N_KV_A else [zeros, qn[hh]]
        return jnp.concatenate(parts, axis=0)

    qa_ref[...] = qnorm_padded(cols("q_a"), qna_ref[...])
    qb_ref[...] = qnorm_padded(cols("q_b"), qnb_ref[...])
    qi_ref[...] = (cols("q_idx") * IDX_SCALE).astype(BF16)
    vs_ref[...] = with_ones(cols("v_slc"))
    vw_ref[...] = with_ones(cols("v_win"))
    gt_ref[...] = jax.nn.sigmoid(cols("g_b") + bg_ref[...])
    wt_ref[...] = cols("w_idx") * IDX_W_SCALE


def _const_spec(shape):
    nd = len(shape)
    return pl.BlockSpec(shape, lambda *_: (0,) * nd)


def _proj_call(x, g, wrow, wcol, kvn, wup, wupv, kna, kns, knw, qna, qnb, bg):
    b, l, _ = x.shape
    t = min(T_PROJ, l)
    grid = (b, l // t)
    row = lambda w: pl.BlockSpec((None, t, w), lambda i, j: (i, j, 0))
    col = lambda r: pl.BlockSpec((None, r, t), lambda i, j: (i, 0, j))
    rs = lambda w, dt: jax.ShapeDtypeStruct((b, l, w), dt)
    cs = lambda r, dt: jax.ShapeDtypeStruct((b, r, l), dt)
    consts = (g, wrow, wcol, kvn, wup, wupv, kna, kns, knw, qna, qnb, bg)
    return pl.pallas_call(
        _proj_kernel,
        grid=grid,
        in_specs=[row(D_MODEL)] + [_const_spec(c.shape) for c in consts],
        out_specs=[row(128), row(IDX_DIM), row(128), row(128), row(128), row(128), row(D_A + D_B),
                   col(2 * D_A), col(N_IDX_HEADS * IDX_DIM), col(2 * D_B), col(VT_ROWS), col(VT_ROWS), col(VT_ROWS),
                   col(N_HEADS_B * 3), col(N_IDX_HEADS)],
        out_shape=[rs(128, BF16), rs(IDX_DIM, BF16), rs(128, BF16), rs(128, BF16), rs(128, F32), rs(128, F32),
                   rs(D_A + D_B, F32),
                   cs(2 * D_A, BF16), cs(N_IDX_HEADS * IDX_DIM, BF16), cs(2 * D_B, BF16), cs(VT_ROWS, BF16),
                   cs(VT_ROWS, BF16), cs(VT_ROWS, BF16), cs(N_HEADS_B * 3, F32), cs(N_IDX_HEADS, F32)],
        compiler_params=pltpu.CompilerParams(dimension_semantics=("arbitrary", "arbitrary"),
                                             vmem_limit_bytes=VMEM_LIMIT),
        name="proj",
    )(x, *consts)


def _compress_kernel(kr_ref, vr_ref, pk_ref, pv_ref, wk1_ref, wv1_ref, wk2_ref, wv2_ref, knc_ref,
                     kc_ref, vct_ref):
    n = kr_ref.shape[0]
    row = lax.broadcasted_iota(jnp.int32, (n, 128), 0)
    valid = row < n - 1

    def branch(r_ref, p_ref, w1_ref, w2_ref):
        r = r_ref[...]
        a = _dot((r + p_ref[0:1, :]).astype(BF16), w1_ref[0])
        b = _dot((r + p_ref[1:2, :]).astype(BF16), w1_ref[1])
        pre = a + pltpu.roll(b, n - 1, axis=0)
        return _dot(jax.nn.gelu(pre).astype(BF16), w2_ref[...])

    kc = _group_rms_rows(branch(kr_ref, pk_ref, wk1_ref, wk2_ref), knc_ref[...])
    kc_ref[...] = jnp.where(valid, kc, 0.0).astype(BF16)
    vc = jnp.where(valid, branch(vr_ref, pv_ref, wv1_ref, wv2_ref), 0.0)
    vct_ref[...] = vc.T.astype(BF16)


def _compress_call(kr, vr, pk, pv, wk1, wv1, wk2, wv2, knc):
    b, n, w = kr.shape
    blk = pl.BlockSpec((None, n, w), lambda i: (i, 0, 0))
    consts = (pk, pv, wk1, wv1, wk2, wv2, knc)
    return pl.pallas_call(
        _compress_kernel,
        grid=(b,),
        in_specs=[blk, blk] + [_const_spec(c.shape) for c in consts],
        out_specs=[pl.BlockSpec((None, n, 128), lambda i: (i, 0, 0)),
                   pl.BlockSpec((None, 128, n), lambda i: (i, 0, 0))],
        out_shape=[jax.ShapeDtypeStruct((b, n, 128), BF16), jax.ShapeDtypeStruct((b, 128, n), BF16)],
        compiler_params=pltpu.CompilerParams(dimension_semantics=("arbitrary",), vmem_limit_bytes=VMEM_LIMIT),
        name="compress",
    )(kr, vr, *consts)


def _flush_tile(vt_prev, p_ref, a_ref, acc_ref, h):
    acc_ref[h] = a_ref[h] * acc_ref[h] + _dot(vt_prev(h), p_ref[h])


def _flush(vt_prev, p_ref, a_ref, acc_ref):
    for h in range(N_HEADS_A):
        _flush_tile(vt_prev, p_ref, a_ref, acc_ref, h)


def _attend_tile(k_t, q_ref, vt_prev, first, bias_of, shift_of, lg_ref, p_ref, m_ref, a_ref, acc_ref):
    heads = range(N_HEADS_A)
    for h in heads:
        lg_ref[h] = _dot(k_t, q_ref[128 * h:128 * (h + 1), :])
    for h in heads:
        _flush_tile(vt_prev, p_ref, a_ref, acc_ref, h)
    for h in heads:
        shift = shift_of(h)
        for c in range(TQ // LANES):
            cs = slice(LANES * c, LANES * (c + 1))
            lg = lg_ref[h, :, cs]
            for term in bias_of(h, c):
                lg = lg + term
            m_old = m_ref[h, :, cs]
            m_new = jnp.maximum(m_old, jnp.max(lg, axis=0, keepdims=True) + shift)
            a_ref[h, :, cs] = jnp.exp2(m_old - m_new)
            m_ref[h, :, cs] = m_new
            p_ref[h, :, cs] = jnp.exp2(lg - (m_new - shift)).astype(BF16)

    @pl.when(first)
    def _():
        acc_ref[...] = jnp.zeros(acc_ref.shape, F32)


def _build_toeplitz(tab_ref, toe_ref):
    period = tab_ref.shape[-1]
    for h in range(toe_ref.shape[0]):
        rows = jnp.broadcast_to(tab_ref[h], (TS, period))
        toe_ref[h] = pltpu.roll(rows, 0, 1, stride=1, stride_axis=0)[:, :toe_ref.shape[-1]]


def _first_step():
    return jnp.logical_and(pl.program_id(0) == 0, pl.program_id(1) == 0)


def _reset(m_ref):
    m_ref[...] = jnp.full(m_ref.shape, NEG_INF, F32)


def _init_staging(p_ref, a_ref, acc_ref):
    p_ref[...] = jnp.zeros(p_ref.shape, BF16)
    a_ref[...] = jnp.ones(a_ref.shape, F32)
    acc_ref[...] = jnp.zeros(acc_ref.shape, F32)


def _normalized(acc_ref, h):
    return acc_ref[h, :HEAD_DIM, :] / acc_ref[h, HEAD_DIM:HEAD_DIM + 1, :]


def _store_transposed(o_ref, head_out):
    for c in range(N_HEADS_A // 2):
        pair = jnp.concatenate([head_out(2 * c), head_out(2 * c + 1)], axis=0)
        o_ref[:, 128 * c:128 * (c + 1)] = pair.T


def _dsa_kernel(far_ref, kidx_ref, ka_ref, vat_ref, qi_ref, qa_ref, wt_ref, tab_ref, o_ref,
                keys_ref, toe_ref, lg_ref, p_ref, m_ref, a_ref, acc_ref, *, k_top):
    @pl.when(_first_step())
    def _():
        _build_toeplitz(tab_ref, toe_ref)
        _init_staging(p_ref, a_ref, acc_ref)

    qi = pl.program_id(1)
    nk = qi + 1
    q0 = qi * TQ
    t_pos = q0 + lax.broadcasted_iota(jnp.int32, (2 * TS, TQ), 1)
    s_off = lax.broadcasted_iota(jnp.int32, (2 * TS, TQ), 0)

    n_pairs = (nk + 1) // 2

    def score_body(kp, carry):
        s0 = pl.multiple_of(kp * (2 * TS), 2 * TS)
        kx = kidx_ref[pl.ds(s0, 2 * TS), :]
        acc = jnp.zeros((2 * TS, TQ), F32)
        for h in range(N_IDX_HEADS):
            d = _dot(kx, qi_ref[IDX_DIM * h:IDX_DIM * (h + 1), :])
            acc = acc + wt_ref[h:h + 1, :] * jnp.maximum(d, 0.0)
        acc = jnp.where(acc == 0.0, 0.0, acc)
        bits = pltpu.bitcast(acc, jnp.int32)
        key = bits ^ ((bits >> 31) & jnp.int32(0x7FFFFFFF))
        keys_ref[pl.ds(s0, 2 * TS), :] = jnp.where(s0 + s_off <= t_pos, key, INT_MIN)
        return carry

    lax.fori_loop(0, n_pairs, score_body, 0)

    def count(pred):
        def body(kp, c):
            s0 = pl.multiple_of(kp * (2 * TS), 2 * TS)
            hit = jnp.where(pred(keys_ref[pl.ds(s0, 2 * TS), :]), 1, 0)
            return c + jnp.sum(hit.reshape(2 * TS // COUNT_ROWS, COUNT_ROWS, TQ), axis=0)
        partial = lax.fori_loop(0, n_pairs, body, jnp.zeros((COUNT_ROWS, TQ), jnp.int32))
        return jnp.sum(partial, axis=0, keepdims=True)

    def bisect(it, thr_u):
        cand_u = thr_u | lax.shift_left(jnp.int32(1), 31 - it)
        cand_s = cand_u ^ INT_MIN
        n_ge = count(lambda tile: tile >= cand_s)
        return jnp.where(n_ge >= k_top, cand_u, thr_u)

    thr_u = lax.fori_loop(0, 32, bisect, jnp.zeros((1, TQ), jnp.int32))
    thr = jnp.maximum(thr_u ^ INT_MIN, INT_MIN + 1)

    n_gt = count(lambda tile: tile > thr)
    n_ge = count(lambda tile: tile >= thr)
    tie = n_ge > k_top
    need = jnp.where(tie, (k_top - n_gt).astype(F32), float(2 ** 30))

    @pl.when(jnp.max(jnp.where(tie, 1, 0)) > 0)
    def _():
        lower = (lax.broadcasted_iota(jnp.int32, (TS, TS), 1)
                 <= lax.broadcasted_iota(jnp.int32, (TS, TS), 0))
        tri = jnp.where(lower, 1.0, 0.0).astype(BF16)

        def body(kt, run):
            s0 = pl.multiple_of(kt * TS, TS)
            tile = keys_ref[pl.ds(s0, TS), :]
            eq = tile == thr
            pref = _dot(tri, jnp.where(eq, 1.0, 0.0).astype(BF16))
            surplus = jnp.where(eq, run + pref, 0.0) > need
            keys_ref[pl.ds(s0, TS), :] = jnp.where(surplus, thr - 1, tile)
            return run + pref[TS - 1:TS, :]

        lax.fori_loop(0, nk, body, jnp.zeros((1, TQ), F32))

    _reset(m_ref)

    def att_tile(near, kt, s_prev):
        s0 = pl.multiple_of(kt * TS, TS)
        off = pl.multiple_of((qi - kt) * TQ, TQ)
        unselected = jnp.where(keys_ref[pl.ds(s0, TS), :] >= thr, 0.0, NEG_INF)
        unsel_of = lambda c: unselected[:, LANES * c:LANES * (c + 1)]
        if near:
            bias_of = lambda h, c: (toe_ref[h, :, pl.ds(pl.multiple_of(off + LANES * c, LANES), LANES)],
                                    unsel_of(c))
            shift_of = lambda h: 0.0
        else:
            bias_of = lambda h, c: (unsel_of(c),)
            shift_of = lambda h: far_ref[h]
        _attend_tile(ka_ref[pl.ds(s0, TS), :], qa_ref, _vt_of(vat_ref, s_prev), kt == 0, bias_of, shift_of,
                     lg_ref, p_ref, m_ref, a_ref, acc_ref)
        return s0

    n_far = jnp.maximum(qi - (FAR_TILES - 1), 0)
    s_last = lax.fori_loop(0, n_far, functools.partial(att_tile, False), jnp.int32(0))
    s_last = lax.fori_loop(n_far, nk, functools.partial(att_tile, True), s_last)
    _flush(_vt_of(vat_ref, s_last), p_ref, a_ref, acc_ref)

    _store_transposed(o_ref, functools.partial(_normalized, acc_ref))


def _vt_of(v_ref, s0):
    heads_per_group = N_HEADS_A // N_KV_A
    s0 = pl.multiple_of(s0, TS)
    return lambda h: v_ref[V_ROWS * (h // heads_per_group):V_ROWS * (h // heads_per_group + 1), pl.ds(s0, TS)]


def _attn_scratch():
    return [pltpu.VMEM((N_HEADS_A, TS, TQ), F32),
            pltpu.VMEM((N_HEADS_A, TS, TQ), BF16),
            pltpu.VMEM((N_HEADS_A, 1, TQ), F32),
            pltpu.VMEM((N_HEADS_A, 1, TQ), F32),
            pltpu.VMEM((N_HEADS_A, V_ROWS, TQ), F32)]


def _dsa_call(far, kidx, ka, vat, qi_t, qa_t, wt, tab):
    b, l, _ = ka.shape
    nq = l // TQ
    assert (l // TS) % 2 == 0
    full_rows = lambda w: pl.BlockSpec((None, l, w), lambda i, j: (i, 0, 0))
    qcol = lambda r: pl.BlockSpec((None, r, TQ), lambda i, j: (i, 0, j))
    return pl.pallas_call(
        functools.partial(_dsa_kernel, k_top=min(TOPK_A, l // 4)),
        grid=(b, nq),
        in_specs=[pl.BlockSpec(memory_space=pltpu.SMEM),
                  full_rows(IDX_DIM), full_rows(128),
                  pl.BlockSpec((None, VT_ROWS, l), lambda i, j: (i, 0, 0)),
                  qcol(N_IDX_HEADS * IDX_DIM), qcol(2 * D_A), qcol(N_IDX_HEADS),
                  _const_spec(tab.shape)],
        out_specs=pl.BlockSpec((None, TQ, D_A), lambda i, j: (i, j, 0)),
        out_shape=jax.ShapeDtypeStruct((b, l, D_A), F32),
        scratch_shapes=[pltpu.VMEM((l, TQ), jnp.int32),
                        pltpu.VMEM((N_HEADS_A, TS, TOE_WIDTH), F32)] + _attn_scratch(),
        compiler_params=pltpu.CompilerParams(dimension_semantics=("arbitrary", "arbitrary"),
                                             vmem_limit_bytes=VMEM_LIMIT),
        name="dsa",
    )(far, kidx, ka, vat, qi_t, qa_t, wt, tab)


def _nsa_kernel(far_ref, kc_ref, vct_ref, ks_ref, vst_ref, kw_ref, vwt_ref, qb_ref, gt_ref,
                tab_ref, tabc_ref, slct_ref, o_ref,
                toe_ref, toew_ref, cbn_ref, lgc_ref, pc_ref, oc_ref, vals_ref, sel_ref, ow_ref,
                lg_ref, p_ref, m_ref, a_ref, acc_ref, *, n_sel):
    n_win = WINDOW // TS

    @pl.when(_first_step())
    def _():
        _build_toeplitz(tab_ref, toe_ref)
        _init_staging(p_ref, a_ref, acc_ref)
        inside = (lax.broadcasted_iota(jnp.int32, (TS, TQ), 1) < lax.broadcasted_iota(jnp.int32, (TS, TQ), 0))
        for h in range(N_HEADS_B):
            toew_ref[h] = jnp.where(inside, toe_ref[h, :, n_win * TQ:(n_win + 1) * TQ], NEG_INF)
            rows = jnp.broadcast_to(tabc_ref[h], (CB_ROWS, TOE_PERIOD))
            cbn_ref[h] = pltpu.roll(rows, 0, 1, stride=CMP_STRIDE, stride_axis=0)[:, :TQ]

    qi = pl.program_id(1)
    nk = qi + 1
    q0 = qi * TQ
    n_c = kc_ref.shape[0]
    n_sb = slct_ref.shape[0]
    heads_per_group = N_HEADS_B // N_KV_B
    cb_row0 = pl.multiple_of(jnp.maximum(qi * CB_STEP - CB_BACK, 0), CB_STEP)
    cb_tab0 = pl.multiple_of(CB_BACK - (qi * CB_STEP - cb_row0), CB_STEP)

    t_c = q0 + lax.broadcasted_iota(jnp.int32, (n_c, TQ), 1)
    end_c = lax.broadcasted_iota(jnp.int32, (n_c, TQ), 0) * CMP_STRIDE + (CMP_BLOCK - 1)
    mask_c = t_c >= end_c
    kc = kc_ref[...]
    slct = slct_ref[...]
    t_b = q0 + lax.broadcasted_iota(jnp.int32, (n_sb, TQ), 1)
    blk = lax.broadcasted_iota(jnp.int32, (n_sb, TQ), 0)
    jt = t_b >> int(math.log2(SLC_BLOCK))
    forced = (blk == 0) | (blk == jt) | (blk == jt - 1)
    for h in range(N_HEADS_B):
        lgc_ref[h] = _dot(kc, qb_ref[128 * h:128 * (h + 1), :]) + far_ref[h]
        lgc_ref[h, pl.ds(cb_row0, CB_NEAR), :] = (lgc_ref[h, pl.ds(cb_row0, CB_NEAR), :]
                                                  + cbn_ref[h, pl.ds(cb_tab0, CB_NEAR), :])
    for h in range(N_HEADS_B):
        lg = jnp.where(mask_c, lgc_ref[h], NEG_INF)
        e = jnp.where(mask_c, jnp.exp2(lg - jnp.max(lg, axis=0, keepdims=True)), 0.0)
        den = jnp.sum(e, axis=0, keepdims=True)
        p = e * jnp.where(den > 0.0, 1.0 / den, 0.0)
        p_hi = p.astype(BF16)
        pc_ref[0, h] = p_hi
        pc_ref[1, h] = (p - p_hi.astype(F32)).astype(BF16)
    for g in range(N_KV_B):
        imp = jnp.zeros((n_sb, TQ), F32)
        for r in range(heads_per_group):
            h = g * heads_per_group + r
            oc_ref[h] = _dot(vct_ref[HEAD_DIM * g:HEAD_DIM * (g + 1), :], pc_ref[0, h])
            imp = imp + _dot(slct, pc_ref[0, h]) + _dot(slct, pc_ref[1, h])
        vals = imp + jnp.where(forced, FORCE_BONUS, 0.0)
        vals = jnp.where(blk * SLC_BLOCK <= t_b, vals, NEG_INF)
        vals_ref[...] = vals
        row8 = lax.broadcasted_iota(jnp.int32, (8, TQ), 0)
        v_rows = [vals[8 * rg:8 * (rg + 1)] for rg in range(n_sb // 8)]
        ranks = [jnp.zeros((8, TQ), jnp.int32) for _ in v_rows]
        for m in range(n_sb):
            vm = jnp.broadcast_to(vals_ref[m:m + 1, :], (8, TQ))
            for rg, v in enumerate(v_rows):
                if 8 * rg > m:
                    ahead = jnp.where(vm >= v, 1, 0)
                elif 8 * rg + 7 <= m:
                    ahead = jnp.where(vm > v, 1, 0)
                else:
                    ahead = jnp.where(row8 > m - 8 * rg, jnp.where(vm >= v, 1, 0), jnp.where(vm > v, 1, 0))
                ranks[rg] = ranks[rg] + ahead
        for rg, rank in enumerate(ranks):
            sel_ref[g, 8 * rg:8 * (rg + 1), :] = jnp.where(rank < n_sel, 0.0, NEG_INF)

    _reset(m_ref)
    blocks_per_tile = TS // SLC_BLOCK

    def sel_tile(near, kt, s_prev):
        s0 = pl.multiple_of(kt * TS, TS)
        off = pl.multiple_of((qi - kt) * TQ, TQ)

        def unsel_of(h, c):
            g = h // heads_per_group
            return jnp.concatenate(
                [jnp.broadcast_to(sel_ref[g, pl.ds(kt * blocks_per_tile + i, 1), :][:, LANES * c:LANES * (c + 1)],
                                  (SLC_BLOCK, LANES)) for i in range(blocks_per_tile)], axis=0)

        if near:
            bias_of = lambda h, c: (toe_ref[h, :, pl.ds(pl.multiple_of(off + LANES * c, LANES), LANES)],
                                    unsel_of(h, c))
            shift_of = lambda h: 0.0
        else:
            bias_of = lambda h, c: (unsel_of(h, c),)
            shift_of = lambda h: far_ref[h]
        _attend_tile(ks_ref[pl.ds(s0, TS), :], qb_ref, _vt_of(vst_ref, s_prev), kt == 0, bias_of, shift_of,
                     lg_ref, p_ref, m_ref, a_ref, acc_ref)
        return s0

    n_far = jnp.maximum(qi - (FAR_TILES - 1), 0)
    s_last = lax.fori_loop(0, n_far, functools.partial(sel_tile, False), jnp.int32(0))
    s_last = lax.fori_loop(n_far, nk, functools.partial(sel_tile, True), s_last)
    _flush(_vt_of(vst_ref, s_last), p_ref, a_ref, acc_ref)
    for h in range(N_HEADS_B):
        ow_ref[h] = (gt_ref[3 * h:3 * h + 1, :] * oc_ref[h]
                     + gt_ref[3 * h + 1:3 * h + 2, :] * _normalized(acc_ref, h))

    _reset(m_ref)
    oldest = jnp.minimum(qi, n_win)

    def win_tile(back):
        s0 = pl.multiple_of((qi - back) * TS, TS)
        s_prev = jnp.maximum(qi - back - 1, 0) * TS
        if back == n_win:
            bias_of = lambda h, c: (toew_ref[h, :, LANES * c:LANES * (c + 1)],)
        else:
            bias_of = lambda h, c: (toe_ref[h, :, back * TQ + LANES * c:back * TQ + LANES * (c + 1)],)
        _attend_tile(kw_ref[pl.ds(s0, TS), :], qb_ref, _vt_of(vwt_ref, s_prev), oldest == back, bias_of,
                     lambda h: 0.0, lg_ref, p_ref, m_ref, a_ref, acc_ref)

    for back in range(n_win, 0, -1):
        pl.when(qi >= back)(functools.partial(win_tile, back))
    win_tile(0)
    _flush(_vt_of(vwt_ref, qi * TS), p_ref, a_ref, acc_ref)

    _store_transposed(o_ref, lambda h: ow_ref[h] + gt_ref[3 * h + 2:3 * h + 3, :] * _normalized(acc_ref, h))


def _nsa_call(far, kc, vct, ks, vst, kw, vwt, qb_t, gt, tab, tabc, slct):
    b, l, _ = ks.shape
    nq = l // TQ
    n_c = kc.shape[1]
    n_sb = slct.shape[0]
    full_rows = pl.BlockSpec((None, l, 128), lambda i, j: (i, 0, 0))
    full_cols = pl.BlockSpec((None, VT_ROWS, l), lambda i, j: (i, 0, 0))
    qcol = lambda r: pl.BlockSpec((None, r, TQ), lambda i, j: (i, 0, j))
    head_buf = lambda rows: pltpu.VMEM((N_HEADS_B, rows, TQ), F32)
    return pl.pallas_call(
        functools.partial(_nsa_kernel, n_sel=min(N_SLC, n_sb)),
        grid=(b, nq),
        in_specs=[pl.BlockSpec(memory_space=pltpu.SMEM),
                  pl.BlockSpec((None, n_c, 128), lambda i, j: (i, 0, 0)),
                  pl.BlockSpec((None, 128, n_c), lambda i, j: (i, 0, 0)),
                  full_rows, full_cols, full_rows, full_cols,
                  qcol(2 * D_B), qcol(N_HEADS_B * 3),
                  _const_spec(tab.shape), _const_spec(tabc.shape), _const_spec(slct.shape)],
        out_specs=pl.BlockSpec((None, TQ, D_B), lambda i, j: (i, j, 0)),
        out_shape=jax.ShapeDtypeStruct((b, l, D_B), F32),
        scratch_shapes=[pltpu.VMEM((N_HEADS_B, TS, TOE_WIDTH), F32),
                        head_buf(TS),
                        head_buf(CB_ROWS),
                        head_buf(n_c),
                        pltpu.VMEM((2, N_HEADS_B, n_c, TQ), BF16),
                        head_buf(HEAD_DIM),
                        pltpu.VMEM((n_sb, TQ), F32),
                        pltpu.VMEM((N_KV_B, n_sb, TQ), F32),
                        head_buf(HEAD_DIM)] + _attn_scratch(),
        compiler_params=pltpu.CompilerParams(dimension_semantics=("arbitrary", "arbitrary"),
                                             vmem_limit_bytes=VMEM_LIMIT),
        name="nsa",
    )(far, kc, vct, ks, vst, kw, vwt, qb_t, gt, tab, tabc, slct)


def _out_kernel(x_ref, oa_ref, ob_ref, sz_ref, wa_ref, wb_ref, y_ref):
    sz = sz_ref[...]
    ya = (oa_ref[...] * sz[:, :D_A]).astype(BF16)
    yb = (ob_ref[...] * sz[:, D_A:]).astype(BF16)
    y_ref[...] = x_ref[...] + _dot(ya, wa_ref[...]) + _dot(yb, wb_ref[...])


def _out_call(x, oa, ob, sz, wa, wb):
    b, l, d = x.shape
    t = min(T_OUT, l)
    row = lambda w: pl.BlockSpec((None, t, w), lambda i, j: (i, j, 0))
    return pl.pallas_call(
        _out_kernel,
        grid=(b, l // t),
        in_specs=[row(d), row(D_A), row(D_B), row(D_A + D_B), _const_spec(wa.shape), _const_spec(wb.shape)],
        out_specs=row(d),
        out_shape=jax.ShapeDtypeStruct((b, l, d), F32),
        compiler_params=pltpu.CompilerParams(dimension_semantics=("arbitrary", "arbitrary"),
                                             vmem_limit_bytes=VMEM_LIMIT),
        name="out",
    )(x, oa, ob, sz, wa, wb)


def _t5_bucket(dist):
    n = jnp.maximum(dist, 0)
    max_exact = N_BUCKETS // 2
    nf = jnp.maximum(n, 1).astype(jnp.float32)
    log_b = max_exact + (jnp.log(nf / max_exact) / math.log(MAX_DISTANCE / max_exact)
                         * (N_BUCKETS - max_exact)).astype(jnp.int32)
    return jnp.where(n < max_exact, n, jnp.minimum(log_b, N_BUCKETS - 1))


def _bias_vectors(rel_bias):
    assert TOE_PERIOD >= TQ + CB_OFFSET and WINDOW % TS == 0 and WINDOW // TS < FAR_TILES
    k = np.arange(TOE_PERIOD)
    by_dist = rel_bias[_t5_bucket(jnp.arange(TOE_PERIOD))].T.astype(F32) * LOG2E
    far = rel_bias[N_BUCKETS - 1].astype(F32) * LOG2E
    tab = jnp.where(jnp.asarray(k < TOE_WIDTH)[None, :], by_dist, NEG_INF)
    dist_c = np.where(k < TQ, k, k - TOE_PERIOD) + CB_OFFSET
    tabc = jnp.where(jnp.asarray(dist_c >= 0)[None, :],
                     by_dist[:, np.clip(dist_c, 0, TOE_PERIOD - 1)] - far[:, None], 0.0)
    head_major = lambda a: a[N_HEADS_A:, None, :]
    return tab[:N_HEADS_A, None, :], head_major(tab), head_major(tabc), far[:N_HEADS_A], far[N_HEADS_A:]


def _overlap_map_t(n_c, n_sb):
    c_start = np.arange(n_c)[None, :] * CMP_STRIDE
    b_start = np.arange(n_sb)[:, None] * SLC_BLOCK
    ov = np.minimum(c_start + CMP_BLOCK, b_start + SLC_BLOCK) - np.maximum(c_start, b_start)
    return (np.maximum(ov, 0) / CMP_BLOCK).astype(np.float32)


def _pick(w, names, pad_to=None):
    parts = [w[:, PROJ_COLS[n][0]:PROJ_COLS[n][1]] for n in names]
    out = jnp.concatenate(parts, axis=1)
    if pad_to is not None and out.shape[1] < pad_to:
        out = jnp.pad(out, ((0, 0), (0, pad_to - out.shape[1])))
    return out


def _compress_weights(w1, w2, pos):
    half = CMP_BLOCK // 2
    eye = jnp.eye(N_KV_B, dtype=w1.dtype)
    w1r = w1.reshape(2, half, HEAD_DIM, HEAD_DIM)
    w1b = jnp.einsum("sjde,gk->sjgdke", w1r, eye).reshape(2, half * N_KV_B * HEAD_DIM, N_KV_B * HEAD_DIM)
    w2b = jnp.einsum("de,gk->gdke", w2, eye).reshape(N_KV_B * HEAD_DIM, N_KV_B * HEAD_DIM)
    posr = jnp.broadcast_to(pos.reshape(2, half, 1, HEAD_DIM), (2, half, N_KV_B, HEAD_DIM)).reshape(2, -1)
    return w1b.astype(BF16), w2b.astype(BF16), posr.astype(F32)


def kernel(x, rel_bias, norm_g, w_in, b_gate_b, kv_norm_a, w_kv_up_a, q_norm_a, k_norm_a, q_norm_b, k_norm_cmp,
           k_norm_slc, k_norm_win, cmp_pos_k, cmp_pos_v, w_cmp_k1, w_cmp_k2, w_cmp_v1, w_cmp_v2, w_out):
    b, l, d = x.shape
    depth = norm_g.shape[0]
    assert d == D_MODEL and l % T_PROJ == 0 and l % TQ == 0 and l // CMP_STRIDE >= CB_NEAR
    n_c = l // CMP_STRIDE
    n_sb = l // SLC_BLOCK
    tab_a, tab_b, tabc_b, far_a, far_b = _bias_vectors(rel_bias)
    slct = jnp.asarray(_overlap_map_t(n_c, n_sb)).astype(BF16)
    slct = slct * jnp.asarray(np.arange(n_c)[None, :] < n_c - 1, BF16)
    two = lambda v: jnp.tile(v, N_KV_B)[None, :].astype(F32)
    colv = lambda v: v[:, None].astype(F32)
    for layer in range(depth):
        w = w_in[layer]
        wrow = _pick(w, ROW_ORDER, ROW_WIDTH).astype(BF16)
        wcol = _pick(w, COL_ORDER).T.astype(BF16)
        wup = w_kv_up_a[layer]
        n_k = N_KV_A * HEAD_DIM
        (ka, kidx, ks, kw, kcmp, vcmp, sz, qa_t, qi_t, qb_t, va_t, vs_t, vw_t, g_t, w_t) = _proj_call(
            x, norm_g[layer][None, :], wrow, wcol, kv_norm_a[layer][None, :],
            wup[:, :n_k].astype(BF16), wup[:, n_k:].T.astype(BF16),
            two(k_norm_a[layer]), two(k_norm_slc[layer]), two(k_norm_win[layer]),
            colv(q_norm_a[layer]), colv(q_norm_b[layer]), colv(b_gate_b[layer]))
        wk1, wk2, pk = _compress_weights(w_cmp_k1[layer], w_cmp_k2[layer], cmp_pos_k[layer])
        wv1, wv2, pv = _compress_weights(w_cmp_v1[layer], w_cmp_v2[layer], cmp_pos_v[layer])
        row16 = CMP_STRIDE * N_KV_B * HEAD_DIM
        kc, vc_t = _compress_call(kcmp.reshape(b, n_c, row16), vcmp.reshape(b, n_c, row16),
                                  pk, pv, wk1, wv1, wk2, wv2, two(k_norm_cmp[layer]))
        o_a = _dsa_call(far_a, kidx, ka, va_t, qi_t, qa_t, w_t, tab_a)
        o_b = _nsa_call(far_b, kc, vc_t, ks, vs_t, kw, vw_t, qb_t, g_t, tab_b, tabc_b, slct)
        wo = w_out[layer].astype(BF16)
        x = _out_call(x, o_a, o_b, sz, wo[:D_A], wo[D_A:])
    return x
```

```python
import functools
import math

import jax
import jax.numpy as jnp
import numpy as np
from jax import lax
from jax.experimental import pallas as pl
from jax.experimental.pallas import tpu as pltpu

D_MODEL = 1024
HEAD_DIM = 64
N_HEADS_A = 8
N_HEADS_B = 8
D_A = N_HEADS_A * HEAD_DIM
D_B = N_HEADS_B * HEAD_DIM
KV_RANK_A = 128
N_KV_A = 2
N_IDX_HEADS = 8
IDX_DIM = 64
TOPK_A = 256
N_KV_B = 2
D_KV_B = N_KV_B * HEAD_DIM
CMP_BLOCK = 32
CMP_STRIDE = 16
SLC_BLOCK = 64
N_SLC = 16
WINDOW = 512
N_BUCKETS = 32
MAX_DISTANCE = 1024
EPS = 1e-6
NEG_INF = -1e30
FORCE_BONUS = 1e4
ATTN_SCALE = HEAD_DIM ** -0.5
IDX_SCALE = IDX_DIM ** -0.5
IDX_W_SCALE = N_IDX_HEADS ** -0.5

PROJ_SIZES = (
    D_A, KV_RANK_A, N_IDX_HEADS * IDX_DIM, IDX_DIM, N_IDX_HEADS, D_A,
    D_B, D_KV_B, D_KV_B, D_KV_B, D_KV_B, D_KV_B, D_KV_B, N_HEADS_B * 3, D_B,
)
PROJ_NAMES = ("q_a", "c_kv", "q_idx", "k_idx", "w_idx", "z_a",
              "q_b", "k_cmp", "v_cmp", "k_slc", "v_slc", "k_win", "v_win", "g_b", "z_b")
_OFFS = np.concatenate([[0], np.cumsum(PROJ_SIZES)])
PROJ_COLS = {n: (int(_OFFS[i]), int(_OFFS[i + 1])) for i, n in enumerate(PROJ_NAMES)}

TQ = 256
TS = 256
T_PROJ = 512
T_OUT = 512
FAR_DIST = MAX_DISTANCE + 1
FAR_TILES = -(-(FAR_DIST + TS - 1) // TQ)
TOE_WIDTH = FAR_TILES * TQ
TOE_PERIOD = TOE_WIDTH + TS
CB_STEP = TQ // CMP_STRIDE
CB_BACK = -(-(FAR_DIST + CMP_BLOCK - 1 - CMP_STRIDE) // (CMP_STRIDE * CB_STEP)) * CB_STEP
CB_NEAR = CB_BACK + CB_STEP
CB_ROWS = CB_BACK + CB_NEAR
CB_OFFSET = CMP_STRIDE * CB_BACK - (CMP_BLOCK - 1)
LOG2E = math.log2(math.e)
V_ROWS = HEAD_DIM + 16
VT_ROWS = N_KV_A * V_ROWS
LANES = 128
COUNT_ROWS = 32
VMEM_LIMIT = 56 * 1024 * 1024
ROW_CHUNK = 64
INT_MIN = -(2 ** 31)

F32 = jnp.float32
BF16 = jnp.bfloat16
NT_DIMS = (((1,), (1,)), ((), ()))

ROW_ORDER = ("c_kv", "z_a", "z_b", "k_cmp", "v_cmp", "k_slc", "k_win", "k_idx")
ROW_WIDTH = 1792
COL_ORDER = ("q_a", "q_idx", "q_b", "v_slc", "v_win", "g_b", "w_idx")


def _layout(order):
    offs, o = {}, 0
    for n in order:
        w = PROJ_COLS[n][1] - PROJ_COLS[n][0]
        offs[n] = (o, o + w)
        o += w
    return offs, o


ROW_OFFS, _ROW_USED = _layout(ROW_ORDER)
COL_OFFS, COL_ROWS = _layout(COL_ORDER)


def _dot(a, b):
    return jnp.dot(a, b, preferred_element_type=F32)


def _dot_nt(a, b):
    return lax.dot_general(a, b, NT_DIMS, preferred_element_type=F32)


def _rms_rows(a, gain):
    ms = jnp.mean(a * a, axis=-1, keepdims=True)
    return a * lax.rsqrt(ms + EPS) * gain


def _group_rms_rows(a, gain):
    lane = lax.broadcasted_iota(jnp.int32, a.shape, 1)
    sq = a * a
    lo = lane < HEAD_DIM
    s_lo = jnp.sum(jnp.where(lo, sq, 0.0), axis=-1, keepdims=True)
    s_hi = jnp.sum(jnp.where(lo, 0.0, sq), axis=-1, keepdims=True)
    ms = jnp.where(lo, s_lo, s_hi) * (1.0 / HEAD_DIM)
    return a * lax.rsqrt(ms + EPS) * gain


def _proj_kernel(x_ref, g_ref, wrow_ref, wcol_ref, kvn_ref, wup_ref, wupv_ref, kna_ref, kns_ref, knw_ref,
                 qna_ref, qnb_ref, bg_ref,
                 ka_ref, kidx_ref, ks_ref, kw_ref, kcmp_ref, vcmp_ref, sz_ref,
                 qa_ref, qi_ref, qb_ref, va_ref, vs_ref, vw_ref, gt_ref, wt_ref):
    t = x_ref.shape[0]
    h = _rms_rows(x_ref[...], g_ref[...]).astype(BF16)
    pr = _dot(h, wrow_ref[...])

    def rows(name):
        a, b = ROW_OFFS[name]
        return pr[:, a:b]

    one_row = jnp.where(lax.broadcasted_iota(jnp.int32, (V_ROWS - HEAD_DIM, t), 0) == 0, 1.0, 0.0).astype(BF16)

    def with_ones(vt):
        vt = vt.astype(BF16)
        return jnp.concatenate([vt[:HEAD_DIM], one_row, vt[HEAD_DIM:], one_row], axis=0)

    ckvn = _rms_rows(rows("c_kv"), kvn_ref[...]).astype(BF16)
    k_up = _dot(ckvn, wup_ref[...])
    ka_ref[...] = _group_rms_rows(k_up, kna_ref[...]).astype(BF16)
    va_ref[...] = with_ones(_dot_nt(wupv_ref[...], ckvn))
    za, zb = ROW_OFFS["z_a"][0], ROW_OFFS["z_b"][1]
    z = pr[:, za:zb]
    sz_ref[...] = z * jax.nn.sigmoid(z)
    kcmp_ref[...] = rows("k_cmp")
    vcmp_ref[...] = rows("v_cmp")
    ks_ref[...] = _group_rms_rows(rows("k_slc"), kns_ref[...]).astype(BF16)
    kw_ref[...] = _group_rms_rows(rows("k_win"), knw_ref[...]).astype(BF16)
    kidx_ref[...] = rows("k_idx").astype(BF16)

    pt = _dot_nt(wcol_ref[...], h)

    def cols(name):
        a, b = COL_OFFS[name]
        return pt[a:b, :]

    def qnorm_padded(q, gain_col):
        q = q.reshape(N_HEADS_A, HEAD_DIM, t)
        ms = jnp.mean(q * q, axis=1, keepdims=True)
        qn = (q * lax.rsqrt(ms + EPS) * gain_col[None] * (ATTN_SCALE * LOG2E)).astype(BF16)
        zeros = jnp.zeros((HEAD_DIM, t), BF16)
        parts = []
        for hh in range(N_HEADS_A):
            parts += [qn[hh], zeros] if hh < N_HEADS_A // N_KV_A else [zeros, qn[hh]]
        return jnp.concatenate(parts, axis=0)

    qa_ref[...] = qnorm_padded(cols("q_a"), qna_ref[...])
    qb_ref[...] = qnorm_padded(cols("q_b"), qnb_ref[...])
    qi_ref[...] = (cols("q_idx") * IDX_SCALE).astype(BF16)
    vs_ref[...] = with_ones(cols("v_slc"))
    vw_ref[...] = with_ones(cols("v_win"))
    gt_ref[...] = jax.nn.sigmoid(cols("g_b") + bg_ref[...])
    wt_ref[...] = cols("w_idx") * IDX_W_SCALE


def _const_spec(shape):
    nd = len(shape)
    return pl.BlockSpec(shape, lambda *_: (0,) * nd)


def _proj_call(x, g, wrow, wcol, kvn, wup, wupv, kna, kns, knw, qna, qnb, bg):
    b, l, _ = x.shape
    t = min(T_PROJ, l)
    grid = (b, l // t)
    row = lambda w: pl.BlockSpec((None, t, w), lambda i, j: (i, j, 0))
    col = lambda r: pl.BlockSpec((None, r, t), lambda i, j: (i, 0, j))
    rs = lambda w, dt: jax.ShapeDtypeStruct((b, l, w), dt)
    cs = lambda r, dt: jax.ShapeDtypeStruct((b, r, l), dt)
    consts = (g, wrow, wcol, kvn, wup, wupv, kna, kns, knw, qna, qnb, bg)
    return pl.pallas_call(
        _proj_kernel,
        grid=grid,
        in_specs=[row(D_MODEL)] + [_const_spec(c.shape) for c in consts],
        out_specs=[row(128), row(IDX_DIM), row(128), row(128), row(128), row(128), row(D_A + D_B),
                   col(2 * D_A), col(N_IDX_HEADS * IDX_DIM), col(2 * D_B), col(VT_ROWS), col(VT_ROWS), col(VT_ROWS),
                   col(N_HEADS_B * 3), col(N_IDX_HEADS)],
        out_shape=[rs(128, BF16), rs(IDX_DIM, BF16), rs(128, BF16), rs(128, BF16), rs(128, F32), rs(128, F32),
                   rs(D_A + D_B, F32),
                   cs(2 * D_A, BF16), cs(N_IDX_HEADS * IDX_DIM, BF16), cs(2 * D_B, BF16), cs(VT_ROWS, BF16),
                   cs(VT_ROWS, BF16), cs(VT_ROWS, BF16), cs(N_HEADS_B * 3, F32), cs(N_IDX_HEADS, F32)],
        compiler_params=pltpu.CompilerParams(dimension_semantics=("arbitrary", "arbitrary"),
                                             vmem_limit_bytes=VMEM_LIMIT),
        name="proj",
    )(x, *consts)


def _compress_kernel(kr_ref, vr_ref, pk_ref, pv_ref, wk1_ref, wv1_ref, wk2_ref, wv2_ref, knc_ref,
                     kc_ref, vct_ref):
    n = kr_ref.shape[0]
    row = lax.broadcasted_iota(jnp.int32, (n, 128), 0)
    valid = row < n - 1

    def branch(r_ref, p_ref, w1_ref, w2_ref):
        r = r_ref[...]
        a = _dot((r + p_ref[0:1, :]).astype(BF16), w1_ref[0])
        b = _dot((r + p_ref[1:2, :]).astype(BF16), w1_ref[1])
        pre = a + pltpu.roll(b, n - 1, axis=0)
        return _dot(jax.nn.gelu(pre).astype(BF16), w2_ref[...])

    kc = _group_rms_rows(branch(kr_ref, pk_ref, wk1_ref, wk2_ref), knc_ref[...])
    kc_ref[...] = jnp.where(valid, kc, 0.0).astype(BF16)
    vc = jnp.where(valid, branch(vr_ref, pv_ref, wv1_ref, wv2_ref), 0.0)
    vct_ref[...] = vc.T.astype(BF16)


def _compress_call(kr, vr, pk, pv, wk1, wv1, wk2, wv2, knc):
    b, n, w = kr.shape
    blk = pl.BlockSpec((None, n, w), lambda i: (i, 0, 0))
    consts = (pk, pv, wk1, wv1, wk2, wv2, knc)
    return pl.pallas_call(
        _compress_kernel,
        grid=(b,),
        in_specs=[blk, blk] + [_const_spec(c.shape) for c in consts],
        out_specs=[pl.BlockSpec((None, n, 128), lambda i: (i, 0, 0)),
                   pl.BlockSpec((None, 128, n), lambda i: (i, 0, 0))],
        out_shape=[jax.ShapeDtypeStruct((b, n, 128), BF16), jax.ShapeDtypeStruct((b, 128, n), BF16)],
        compiler_params=pltpu.CompilerParams(dimension_semantics=("arbitrary",), vmem_limit_bytes=VMEM_LIMIT),
        name="compress",
    )(kr, vr, *consts)


def _flush_tile(vt_prev, p_ref, a_ref, acc_ref, h):
    acc_ref[h] = a_ref[h] * acc_ref[h] + _dot(vt_prev(h), p_ref[h])


def _flush(vt_prev, p_ref, a_ref, acc_ref):
    for h in range(N_HEADS_A):
        _flush_tile(vt_prev, p_ref, a_ref, acc_ref, h)


def _attend_tile(k_t, q_ref, vt_prev, first, bias_of, shift_of, lg_ref, p_ref, m_ref, a_ref, acc_ref):
    heads = range(N_HEADS_A)
    for h in heads:
        lg_ref[h] = _dot(k_t, q_ref[128 * h:128 * (h + 1), :])
    for h in heads:
        _flush_tile(vt_prev, p_ref, a_ref, acc_ref, h)
    for h in heads:
        shift = shift_of(h)
        for c in range(TQ // LANES):
            cs = slice(LANES * c, LANES * (c + 1))
            terms = bias_of(h, c)
            top = None
            for r in range(TS // ROW_CHUNK):
                rs = slice(ROW_CHUNK * r, ROW_CHUNK * (r + 1))
                lg = lg_ref[h, rs, cs]
                for term in terms:
                    lg = lg + term[rs]
                lg_ref[h, rs, cs] = lg
                peak = jnp.max(lg, axis=0, keepdims=True)
                top = peak if top is None else jnp.maximum(top, peak)
            m_old = m_ref[h, :, cs]
            m_new = jnp.maximum(m_old, top + shift)
            a_ref[h, :, cs] = jnp.exp2(m_old - m_new)
            m_ref[h, :, cs] = m_new
            base = m_new - shift
            for r in range(TS // ROW_CHUNK):
                rs = slice(ROW_CHUNK * r, ROW_CHUNK * (r + 1))
                p_ref[h, rs, cs] = jnp.exp2(lg_ref[h, rs, cs] - base).astype(BF16)

    @pl.when(first)
    def _():
        acc_ref[...] = jnp.zeros(acc_ref.shape, F32)


def _build_toeplitz(tab_ref, toe_ref):
    period = tab_ref.shape[-1]
    for h in range(toe_ref.shape[0]):
        rows = jnp.broadcast_to(tab_ref[h], (TS, period))
        toe_ref[h] = pltpu.roll(rows, 0, 1, stride=1, stride_axis=0)[:, :toe_ref.shape[-1]]


def _first_step():
    return jnp.logical_and(pl.program_id(0) == 0, pl.program_id(1) == 0)


def _reset(m_ref):
    m_ref[...] = jnp.full(m_ref.shape, NEG_INF, F32)


def _init_staging(p_ref, a_ref, acc_ref):
    p_ref[...] = jnp.zeros(p_ref.shape, BF16)
    a_ref[...] = jnp.ones(a_ref.shape, F32)
    acc_ref[...] = jnp.zeros(acc_ref.shape, F32)


def _normalized(acc_ref, h):
    return acc_ref[h, :HEAD_DIM, :] / acc_ref[h, HEAD_DIM:HEAD_DIM + 1, :]


def _store_transposed(o_ref, head_out):
    for c in range(N_HEADS_A // 2):
        pair = jnp.concatenate([head_out(2 * c), head_out(2 * c + 1)], axis=0)
        o_ref[:, 128 * c:128 * (c + 1)] = pair.T


def _dsa_kernel(far_ref, kidx_ref, ka_ref, vat_ref, qi_ref, qa_ref, wt_ref, tab_ref, o_ref,
                keys_ref, toe_ref, lg_ref, p_ref, m_ref, a_ref, acc_ref, *, k_top):
    @pl.when(_first_step())
    def _():
        _build_toeplitz(tab_ref, toe_ref)
        _init_staging(p_ref, a_ref, acc_ref)

    qi = pl.program_id(1)
    nk = qi + 1
    q0 = qi * TQ
    t_pos = q0 + lax.broadcasted_iota(jnp.int32, (2 * TS, TQ), 1)
    s_off = lax.broadcasted_iota(jnp.int32, (2 * TS, TQ), 0)

    n_pairs = (nk + 1) // 2

    def score_body(kp, carry):
        s0 = pl.multiple_of(kp * (2 * TS), 2 * TS)
        kx = kidx_ref[pl.ds(s0, 2 * TS), :]
        acc = jnp.zeros((2 * TS, TQ), F32)
        for h in range(N_IDX_HEADS):
            d = _dot(kx, qi_ref[IDX_DIM * h:IDX_DIM * (h + 1), :])
            acc = acc + wt_ref[h:h + 1, :] * jnp.maximum(d, 0.0)
        acc = jnp.where(acc == 0.0, 0.0, acc)
        bits = pltpu.bitcast(acc, jnp.int32)
        key = bits ^ ((bits >> 31) & jnp.int32(0x7FFFFFFF))
        keys_ref[pl.ds(s0, 2 * TS), :] = jnp.where(s0 + s_off <= t_pos, key, INT_MIN)
        return carry

    lax.fori_loop(0, n_pairs, score_body, 0)

    def count(pred):
        def body(kp, c):
            s0 = pl.multiple_of(kp * (2 * TS), 2 * TS)
            hit = jnp.where(pred(keys_ref[pl.ds(s0, 2 * TS), :]), 1, 0)
            return c + jnp.sum(hit.reshape(2 * TS // COUNT_ROWS, COUNT_ROWS, TQ), axis=0)
        partial = lax.fori_loop(0, n_pairs, body, jnp.zeros((COUNT_ROWS, TQ), jnp.int32))
        return jnp.sum(partial, axis=0, keepdims=True)

    def bisect(it, thr_u):
        cand_u = thr_u | lax.shift_left(jnp.int32(1), 31 - it)
        cand_s = cand_u ^ INT_MIN
        n_ge = count(lambda tile: tile >= cand_s)
        return jnp.where(n_ge >= k_top, cand_u, thr_u)

    thr_u = lax.fori_loop(0, 32, bisect, jnp.zeros((1, TQ), jnp.int32))
    thr = jnp.maximum(thr_u ^ INT_MIN, INT_MIN + 1)

    n_gt = count(lambda tile: tile > thr)
    n_ge = count(lambda tile: tile >= thr)
    tie = n_ge > k_top
    need = jnp.where(tie, (k_top - n_gt).astype(F32), float(2 ** 30))

    @pl.when(jnp.max(jnp.where(tie, 1, 0)) > 0)
    def _():
        lower = (lax.broadcasted_iota(jnp.int32, (TS, TS), 1)
                 <= lax.broadcasted_iota(jnp.int32, (TS, TS), 0))
        tri = jnp.where(lower, 1.0, 0.0).astype(BF16)

        def body(kt, run):
            s0 = pl.multiple_of(kt * TS, TS)
            tile = keys_ref[pl.ds(s0, TS), :]
            eq = tile == thr
            pref = _dot(tri, jnp.where(eq, 1.0, 0.0).astype(BF16))
            surplus = jnp.where(eq, run + pref, 0.0) > need
            keys_ref[pl.ds(s0, TS), :] = jnp.where(surplus, thr - 1, tile)
            return run + pref[TS - 1:TS, :]

        lax.fori_loop(0, nk, body, jnp.zeros((1, TQ), F32))

    _reset(m_ref)

    def att_tile(near, kt, s_prev):
        s0 = pl.multiple_of(kt * TS, TS)
        off = pl.multiple_of((qi - kt) * TQ, TQ)
        unselected = jnp.where(keys_ref[pl.ds(s0, TS), :] >= thr, 0.0, NEG_INF)
        unsel_of = lambda c: unselected[:, LANES * c:LANES * (c + 1)]
        if near:
            bias_of = lambda h, c: (toe_ref[h, :, pl.ds(pl.multiple_of(off + LANES * c, LANES), LANES)],
                                    unsel_of(c))
            shift_of = lambda h: 0.0
        else:
            bias_of = lambda h, c: (unsel_of(c),)
            shift_of = lambda h: far_ref[h]
        _attend_tile(ka_ref[pl.ds(s0, TS), :], qa_ref, _vt_of(vat_ref, s_prev), kt == 0, bias_of, shift_of,
                     lg_ref, p_ref, m_ref, a_ref, acc_ref)
        return s0

    n_far = jnp.maximum(qi - (FAR_TILES - 1), 0)
    s_last = lax.fori_loop(0, n_far, functools.partial(att_tile, False), jnp.int32(0))
    s_last = lax.fori_loop(n_far, nk, functools.partial(att_tile, True), s_last)
    _flush(_vt_of(vat_ref, s_last), p_ref, a_ref, acc_ref)

    _store_transposed(o_ref, functools.partial(_normalized, acc_ref))


def _vt_of(v_ref, s0):
    heads_per_group = N_HEADS_A // N_KV_A
    s0 = pl.multiple_of(s0, TS)
    return lambda h: v_ref[V_ROWS * (h // heads_per_group):V_ROWS * (h // heads_per_group + 1), pl.ds(s0, TS)]


def _attn_scratch():
    return [pltpu.VMEM((N_HEADS_A, TS, TQ), F32),
            pltpu.VMEM((N_HEADS_A, TS, TQ), BF16),
            pltpu.VMEM((N_HEADS_A, 1, TQ), F32),
            pltpu.VMEM((N_HEADS_A, 1, TQ), F32),
            pltpu.VMEM((N_HEADS_A, V_ROWS, TQ), F32)]


def _dsa_call(far, kidx, ka, vat, qi_t, qa_t, wt, tab):
    b, l, _ = ka.shape
    nq = l // TQ
    assert (l // TS) % 2 == 0
    full_rows = lambda w: pl.BlockSpec((None, l, w), lambda i, j: (i, 0, 0))
    qcol = lambda r: pl.BlockSpec((None, r, TQ), lambda i, j: (i, 0, j))
    return pl.pallas_call(
        functools.partial(_dsa_kernel, k_top=min(TOPK_A, l // 4)),
        grid=(b, nq),
        in_specs=[pl.BlockSpec(memory_space=pltpu.SMEM),
                  full_rows(IDX_DIM), full_rows(128),
                  pl.BlockSpec((None, VT_ROWS, l), lambda i, j: (i, 0, 0)),
                  qcol(N_IDX_HEADS * IDX_DIM), qcol(2 * D_A), qcol(N_IDX_HEADS),
                  _const_spec(tab.shape)],
        out_specs=pl.BlockSpec((None, TQ, D_A), lambda i, j: (i, j, 0)),
        out_shape=jax.ShapeDtypeStruct((b, l, D_A), F32),
        scratch_shapes=[pltpu.VMEM((l, TQ), jnp.int32),
                        pltpu.VMEM((N_HEADS_A, TS, TOE_WIDTH), F32)] + _attn_scratch(),
        compiler_params=pltpu.CompilerParams(dimension_semantics=("arbitrary", "arbitrary"),
                                             vmem_limit_bytes=VMEM_LIMIT),
        name="dsa",
    )(far, kidx, ka, vat, qi_t, qa_t, wt, tab)


def _nsa_kernel(far_ref, kc_ref, vct_ref, ks_ref, vst_ref, kw_ref, vwt_ref, qb_ref, gt_ref,
                tab_ref, tabc_ref, slct_ref, o_ref,
                toe_ref, toew_ref, cbn_ref, lgc_ref, pc_ref, oc_ref, vals_ref, sel_ref, ow_ref,
                lg_ref, p_ref, m_ref, a_ref, acc_ref, *, n_sel):
    n_win = WINDOW // TS

    @pl.when(_first_step())
    def _():
        _build_toeplitz(tab_ref, toe_ref)
        _init_staging(p_ref, a_ref, acc_ref)
        inside = (lax.broadcasted_iota(jnp.int32, (TS, TQ), 1) < lax.broadcasted_iota(jnp.int32, (TS, TQ), 0))
        for h in range(N_HEADS_B):
            toew_ref[h] = jnp.where(inside, toe_ref[h, :, n_win * TQ:(n_win + 1) * TQ], NEG_INF)
            rows = jnp.broadcast_to(tabc_ref[h], (CB_ROWS, TOE_PERIOD))
            cbn_ref[h] = pltpu.roll(rows, 0, 1, stride=CMP_STRIDE, stride_axis=0)[:, :TQ]

    qi = pl.program_id(1)
    nk = qi + 1
    q0 = qi * TQ
    n_c = kc_ref.shape[0]
    n_sb = slct_ref.shape[0]
    heads_per_group = N_HEADS_B // N_KV_B
    cb_row0 = pl.multiple_of(jnp.maximum(qi * CB_STEP - CB_BACK, 0), CB_STEP)
    cb_tab0 = pl.multiple_of(CB_BACK - (qi * CB_STEP - cb_row0), CB_STEP)

    t_c = q0 + lax.broadcasted_iota(jnp.int32, (n_c, TQ), 1)
    end_c = lax.broadcasted_iota(jnp.int32, (n_c, TQ), 0) * CMP_STRIDE + (CMP_BLOCK - 1)
    mask_c = t_c >= end_c
    kc = kc_ref[...]
    slct = slct_ref[...]
    t_b = q0 + lax.broadcasted_iota(jnp.int32, (n_sb, TQ), 1)
    blk = lax.broadcasted_iota(jnp.int32, (n_sb, TQ), 0)
    jt = t_b >> int(math.log2(SLC_BLOCK))
    forced = (blk == 0) | (blk == jt) | (blk == jt - 1)
    for h in range(N_HEADS_B):
        lgc_ref[h] = _dot(kc, qb_ref[128 * h:128 * (h + 1), :]) + far_ref[h]
        lgc_ref[h, pl.ds(cb_row0, CB_NEAR), :] = (lgc_ref[h, pl.ds(cb_row0, CB_NEAR), :]
                                                  + cbn_ref[h, pl.ds(cb_tab0, CB_NEAR), :])
    for h in range(N_HEADS_B):
        lg = jnp.where(mask_c, lgc_ref[h], NEG_INF)
        e = jnp.where(mask_c, jnp.exp2(lg - jnp.max(lg, axis=0, keepdims=True)), 0.0)
        den = jnp.sum(e, axis=0, keepdims=True)
        p = e * jnp.where(den > 0.0, 1.0 / den, 0.0)
        p_hi = p.astype(BF16)
        pc_ref[0, h] = p_hi
        pc_ref[1, h] = (p - p_hi.astype(F32)).astype(BF16)
    for g in range(N_KV_B):
        imp = jnp.zeros((n_sb, TQ), F32)
        for r in range(heads_per_group):
            h = g * heads_per_group + r
            oc_ref[h] = _dot(vct_ref[HEAD_DIM * g:HEAD_DIM * (g + 1), :], pc_ref[0, h])
            imp = imp + _dot(slct, pc_ref[0, h]) + _dot(slct, pc_ref[1, h])
        vals = imp + jnp.where(forced, FORCE_BONUS, 0.0)
        vals = jnp.where(blk * SLC_BLOCK <= t_b, vals, NEG_INF)
        vals_ref[...] = vals
        row8 = lax.broadcasted_iota(jnp.int32, (8, TQ), 0)
        v_rows = [vals[8 * rg:8 * (rg + 1)] for rg in range(n_sb // 8)]
        ranks = [jnp.zeros((8, TQ), jnp.int32) for _ in v_rows]
        for m in range(n_sb):
            vm = jnp.broadcast_to(vals_ref[m:m + 1, :], (8, TQ))
            for rg, v in enumerate(v_rows):
                if 8 * rg > m:
                    ahead = jnp.where(vm >= v, 1, 0)
                elif 8 * rg + 7 <= m:
                    ahead = jnp.where(vm > v, 1, 0)
                else:
                    ahead = jnp.where(row8 > m - 8 * rg, jnp.where(vm >= v, 1, 0), jnp.where(vm > v, 1, 0))
                ranks[rg] = ranks[rg] + ahead
        for rg, rank in enumerate(ranks):
            sel_ref[g, 8 * rg:8 * (rg + 1), :] = jnp.where(rank < n_sel, 0.0, NEG_INF)

    _reset(m_ref)
    blocks_per_tile = TS // SLC_BLOCK

    def sel_tile(near, kt, s_prev):
        s0 = pl.multiple_of(kt * TS, TS)
        off = pl.multiple_of((qi - kt) * TQ, TQ)

        def unsel_of(h, c):
            g = h // heads_per_group
            return jnp.concatenate(
                [jnp.broadcast_to(sel_ref[g, pl.ds(kt * blocks_per_tile + i, 1), :][:, LANES * c:LANES * (c + 1)],
                                  (SLC_BLOCK, LANES)) for i in range(blocks_per_tile)], axis=0)

        if near:
            bias_of = lambda h, c: (toe_ref[h, :, pl.ds(pl.multiple_of(off + LANES * c, LANES), LANES)],
                                    unsel_of(h, c))
            shift_of = lambda h: 0.0
        else:
            bias_of = lambda h, c: (unsel_of(h, c),)
            shift_of = lambda h: far_ref[h]
        _attend_tile(ks_ref[pl.ds(s0, TS), :], qb_ref, _vt_of(vst_ref, s_prev), kt == 0, bias_of, shift_of,
                     lg_ref, p_ref, m_ref, a_ref, acc_ref)
        return s0

    n_far = jnp.maximum(qi - (FAR_TILES - 1), 0)
    s_last = lax.fori_loop(0, n_far, functools.partial(sel_tile, False), jnp.int32(0))
    s_last = lax.fori_loop(n_far, nk, functools.partial(sel_tile, True), s_last)
    _flush(_vt_of(vst_ref, s_last), p_ref, a_ref, acc_ref)
    for h in range(N_HEADS_B):
        ow_ref[h] = (gt_ref[3 * h:3 * h + 1, :] * oc_ref[h]
                     + gt_ref[3 * h + 1:3 * h + 2, :] * _normalized(acc_ref, h))

    _reset(m_ref)
    oldest = jnp.minimum(qi, n_win)

    def win_tile(back):
        s0 = pl.multiple_of((qi - back) * TS, TS)
        s_prev = jnp.maximum(qi - back - 1, 0) * TS
        if back == n_win:
            bias_of = lambda h, c: (toew_ref[h, :, LANES * c:LANES * (c + 1)],)
        else:
            bias_of = lambda h, c: (toe_ref[h, :, back * TQ + LANES * c:back * TQ + LANES * (c + 1)],)
        _attend_tile(kw_ref[pl.ds(s0, TS), :], qb_ref, _vt_of(vwt_ref, s_prev), oldest == back, bias_of,
                     lambda h: 0.0, lg_ref, p_ref, m_ref, a_ref, acc_ref)

    for back in range(n_win, 0, -1):
        pl.when(qi >= back)(functools.partial(win_tile, back))
    win_tile(0)
    _flush(_vt_of(vwt_ref, qi * TS), p_ref, a_ref, acc_ref)

    _store_transposed(o_ref, lambda h: ow_ref[h] + gt_ref[3 * h + 2:3 * h + 3, :] * _normalized(acc_ref, h))


def _nsa_call(far, kc, vct, ks, vst, kw, vwt, qb_t, gt, tab, tabc, slct):
    b, l, _ = ks.shape
    nq = l // TQ
    n_c = kc.shape[1]
    n_sb = slct.shape[0]
    full_rows = pl.BlockSpec((None, l, 128), lambda i, j: (i, 0, 0))
    full_cols = pl.BlockSpec((None, VT_ROWS, l), lambda i, j: (i, 0, 0))
    qcol = lambda r: pl.BlockSpec((None, r, TQ), lambda i, j: (i, 0, j))
    head_buf = lambda rows: pltpu.VMEM((N_HEADS_B, rows, TQ), F32)
    return pl.pallas_call(
        functools.partial(_nsa_kernel, n_sel=min(N_SLC, n_sb)),
        grid=(b, nq),
        in_specs=[pl.BlockSpec(memory_space=pltpu.SMEM),
                  pl.BlockSpec((None, n_c, 128), lambda i, j: (i, 0, 0)),
                  pl.BlockSpec((None, 128, n_c), lambda i, j: (i, 0, 0)),
                  full_rows, full_cols, full_rows, full_cols,
                  qcol(2 * D_B), qcol(N_HEADS_B * 3),
                  _const_spec(tab.shape), _const_spec(tabc.shape), _const_spec(slct.shape)],
        out_specs=pl.BlockSpec((None, TQ, D_B), lambda i, j: (i, j, 0)),
        out_shape=jax.ShapeDtypeStruct((b, l, D_B), F32),
        scratch_shapes=[pltpu.VMEM((N_HEADS_B, TS, TOE_WIDTH), F32),
                        head_buf(TS),
                        head_buf(CB_ROWS),
                        head_buf(n_c),
                        pltpu.VMEM((2, N_HEADS_B, n_c, TQ), BF16),
                        head_buf(HEAD_DIM),
                        pltpu.VMEM((n_sb, TQ), F32),
                        pltpu.VMEM((N_KV_B, n_sb, TQ), F32),
                        head_buf(HEAD_DIM)] + _attn_scratch(),
        compiler_params=pltpu.CompilerParams(dimension_semantics=("arbitrary", "arbitrary"),
                                             vmem_limit_bytes=VMEM_LIMIT),
        name="nsa",
    )(far, kc, vct, ks, vst, kw, vwt, qb_t, gt, tab, tabc, slct)


def _out_kernel(x_ref, oa_ref, ob_ref, sz_ref, wa_ref, wb_ref, y_ref):
    sz = sz_ref[...]
    ya = (oa_ref[...] * sz[:, :D_A]).astype(BF16)
    yb = (ob_ref[...] * sz[:, D_A:]).astype(BF16)
    y_ref[...] = x_ref[...] + _dot(ya, wa_ref[...]) + _dot(yb, wb_ref[...])


def _out_call(x, oa, ob, sz, wa, wb):
    b, l, d = x.shape
    t = min(T_OUT, l)
    row = lambda w: pl.BlockSpec((None, t, w), lambda i, j: (i, j, 0))
    return pl.pallas_call(
        _out_kernel,
        grid=(b, l // t),
        in_specs=[row(d), row(D_A), row(D_B), row(D_A + D_B), _const_spec(wa.shape), _const_spec(wb.shape)],
        out_specs=row(d),
        out_shape=jax.ShapeDtypeStruct((b, l, d), F32),
        compiler_params=pltpu.CompilerParams(dimension_semantics=("arbitrary", "arbitrary"),
                                             vmem_limit_bytes=VMEM_LIMIT),
        name="out",
    )(x, oa, ob, sz, wa, wb)


def _t5_bucket(dist):
    n = jnp.maximum(dist, 0)
    max_exact = N_BUCKETS // 2
    nf = jnp.maximum(n, 1).astype(jnp.float32)
    log_b = max_exact + (jnp.log(nf / max_exact) / math.log(MAX_DISTANCE / max_exact)
                         * (N_BUCKETS - max_exact)).astype(jnp.int32)
    return jnp.where(n < max_exact, n, jnp.minimum(log_b, N_BUCKETS - 1))


def _bias_vectors(rel_bias):
    assert TOE_PERIOD >= TQ + CB_OFFSET and WINDOW % TS == 0 and WINDOW // TS < FAR_TILES
    k = np.arange(TOE_PERIOD)
    by_dist = rel_bias[_t5_bucket(jnp.arange(TOE_PERIOD))].T.astype(F32) * LOG2E
    far = rel_bias[N_BUCKETS - 1].astype(F32) * LOG2E
    tab = jnp.where(jnp.asarray(k < TOE_WIDTH)[None, :], by_dist, NEG_INF)
    dist_c = np.where(k < TQ, k, k - TOE_PERIOD) + CB_OFFSET
    tabc = jnp.where(jnp.asarray(dist_c >= 0)[None, :],
                     by_dist[:, np.clip(dist_c, 0, TOE_PERIOD - 1)] - far[:, None], 0.0)
    head_major = lambda a: a[N_HEADS_A:, None, :]
    return tab[:N_HEADS_A, None, :], head_major(tab), head_major(tabc), far[:N_HEADS_A], far[N_HEADS_A:]


def _overlap_map_t(n_c, n_sb):
    c_start = np.arange(n_c)[None, :] * CMP_STRIDE
    b_start = np.arange(n_sb)[:, None] * SLC_BLOCK
    ov = np.minimum(c_start + CMP_BLOCK, b_start + SLC_BLOCK) - np.maximum(c_start, b_start)
    return (np.maximum(ov, 0) / CMP_BLOCK).astype(np.float32)


def _pick(w, names, pad_to=None):
    parts = [w[:, PROJ_COLS[n][0]:PROJ_COLS[n][1]] for n in names]
    out = jnp.concatenate(parts, axis=1)
    if pad_to is not None and out.shape[1] < pad_to:
        out = jnp.pad(out, ((0, 0), (0, pad_to - out.shape[1])))
    return out


def _compress_weights(w1, w2, pos):
    half = CMP_BLOCK // 2
    eye = jnp.eye(N_KV_B, dtype=w1.dtype)
    w1r = w1.reshape(2, half, HEAD_DIM, HEAD_DIM)
    w1b = jnp.einsum("sjde,gk->sjgdke", w1r, eye).reshape(2, half * N_KV_B * HEAD_DIM, N_KV_B * HEAD_DIM)
    w2b = jnp.einsum("de,gk->gdke", w2, eye).reshape(N_KV_B * HEAD_DIM, N_KV_B * HEAD_DIM)
    posr = jnp.broadcast_to(pos.reshape(2, half, 1, HEAD_DIM), (2, half, N_KV_B, HEAD_DIM)).reshape(2, -1)
    return w1b.astype(BF16), w2b.astype(BF16), posr.astype(F32)


def kernel(x, rel_bias, norm_g, w_in, b_gate_b, kv_norm_a, w_kv_up_a, q_norm_a, k_norm_a, q_norm_b, k_norm_cmp,
           k_norm_slc, k_norm_win, cmp_pos_k, cmp_pos_v, w_cmp_k1, w_cmp_k2, w_cmp_v1, w_cmp_v2, w_out):
    b, l, d = x.shape
    depth = norm_g.shape[0]
    assert d == D_MODEL and l % T_PROJ == 0 and l % TQ == 0 and l // CMP_STRIDE >= CB_NEAR
    n_c = l // CMP_STRIDE
    n_sb = l // SLC_BLOCK
    tab_a, tab_b, tabc_b, far_a, far_b = _bias_vectors(rel_bias)
    slct = jnp.asarray(_overlap_map_t(n_c, n_sb)).astype(BF16)
    slct = slct * jnp.asarray(np.arange(n_c)[None, :] < n_c - 1, BF16)
    two = lambda v: jnp.tile(v, N_KV_B)[None, :].astype(F32)
    colv = lambda v: v[:, None].astype(F32)
    for layer in range(depth):
        w = w_in[layer]
        wrow = _pick(w, ROW_ORDER, ROW_WIDTH).astype(BF16)
        wcol = _pick(w, COL_ORDER).T.astype(BF16)
        wup = w_kv_up_a[layer]
        n_k = N_KV_A * HEAD_DIM
        (ka, kidx, ks, kw, kcmp, vcmp, sz, qa_t, qi_t, qb_t, va_t, vs_t, vw_t, g_t, w_t) = _proj_call(
            x, norm_g[layer][None, :], wrow, wcol, kv_norm_a[layer][None, :],
            wup[:, :n_k].astype(BF16), wup[:, n_k:].T.astype(BF16),
            two(k_norm_a[layer]), two(k_norm_slc[layer]), two(k_norm_win[layer]),
            colv(q_norm_a[layer]), colv(q_norm_b[layer]), colv(b_gate_b[layer]))
        wk1, wk2, pk = _compress_weights(w_cmp_k1[layer], w_cmp_k2[layer], cmp_pos_k[layer])
        wv1, wv2, pv = _compress_weights(w_cmp_v1[layer], w_cmp_v2[layer], cmp_pos_v[layer])
        row16 = CMP_STRIDE * N_KV_B * HEAD_DIM
        kc, vc_t = _compress_call(kcmp.reshape(b, n_c, row16), vcmp.reshape(b, n_c, row16),
                                  pk, pv, wk1, wv1, wk2, wv2, two(k_norm_cmp[layer]))
        o_a = _dsa_call(far_a, kidx, ka, va_t, qi_t, qa_t, w_t, tab_a)
        o_b = _nsa_call(far_b, kc, vc_t, ks, vs_t, kw, vw_t, qb_t, g_t, tab_b, tabc_b, slct)
        wo = w_out[layer].astype(BF16)
        x = _out_call(x, o_a, o_b, sz, wo[:D_A], wo[D_A:])
    return x
```
